```python
import math
import jax, jax.numpy as jnp
from jax import lax
import numpy as np

D_MODEL = 4096
BATCH = 2
SEQ = 8192
DEPTH = 4
DEC_BATCH = 4
DEC_SEQ = 4096
PAST_LEN = 128

A_HEAD_DIM = 128
A_HEADS = D_MODEL // 256
A_WIDTH = A_HEADS * A_HEAD_DIM
CONV_K = 5
CHUNK = 64
DT_MIN = 1e-3
DT_MAX = 1e-1
B_GROUPS = ((128, 1), (512, 4), (2048, 16))
N_GROUPS = len(B_GROUPS)
B_HEAD_DIM = 128
B_HEADS = D_MODEL // 512
B_GROUP_WIDTH = B_HEADS * B_HEAD_DIM
B_QKV_WIDTH = N_GROUPS * B_GROUP_WIDTH
ROPE_THETA = 10000.0
NEG_INF = -1e30
EPS = 1e-6
L2_EPS = 1e-6
ADA_INIT = 0.5
SPLIT_SIZES = (A_WIDTH, A_WIDTH, A_WIDTH, A_WIDTH, 2 * A_HEADS, 2 * A_HEADS,
               B_QKV_WIDTH, B_QKV_WIDTH, B_QKV_WIDTH, B_GROUP_WIDTH, 2 * D_MODEL)
N_IN = sum(SPLIT_SIZES)
SPLIT_POINTS = tuple(int(s) for s in np.cumsum(SPLIT_SIZES)[:-1])

kernel_name = 'hybrid_gdn_dilated_window_encoder'


def _rms_norm(x, gain):
    xf = x.astype(jnp.float32)
    y = xf * lax.rsqrt(jnp.mean(xf * xf, axis=-1, keepdims=True) + EPS)
    return (y * gain.astype(jnp.float32)).astype(x.dtype)


def _l2norm(t):
    return t * lax.rsqrt(jnp.sum(t * t, axis=-1, keepdims=True) + L2_EPS)


def _centred_dwconv(x, w):
    pad = CONV_K // 2
    return lax.conv_general_dilated(
        x, w[:, None, :].astype(x.dtype), window_strides=(1,),
        padding=[(pad, pad)], dimension_numbers=('NWC', 'WIO', 'NWC'),
        feature_group_count=x.shape[-1])


def _rope(t, pos):
    half = t.shape[-1] // 2
    inv = ROPE_THETA ** (-jnp.arange(half, dtype=jnp.float32) / half)
    ang = pos.astype(jnp.float32)[:, None] * inv[None, :]
    cos = jnp.cos(ang)[None, :, None, :]
    sin = jnp.sin(ang)[None, :, None, :]
    t1, t2 = t[..., :half], t[..., half:]
    return jnp.concatenate([t1 * cos - t2 * sin, t1 * sin + t2 * cos], axis=-1)


def _chunk_gated_delta(q, k, v, g, beta):
    Bn, S, H, dk = q.shape
    dv = v.shape[-1]
    nc = S // CHUNK

    def to_chunks(t):
        return jnp.moveaxis(t, 2, 1).reshape((Bn, H, nc, CHUNK) + t.shape[3:])

    q, k, v, g, beta = (to_chunks(t) for t in (q, k, v, g, beta))
    g = jnp.cumsum(g, axis=-1)
    idx = jnp.arange(CHUNK)
    lower = idx[:, None] >= idx[None, :]
    strict = idx[:, None] > idx[None, :]
    decay = jnp.exp(jnp.where(lower, g[..., :, None] - g[..., None, :], -jnp.inf))
    kb = k * beta[..., None]
    L = jnp.where(strict, jnp.einsum('bhncd,bhnmd->bhncm', kb, k) * decay, 0.0)
    eye = jnp.eye(CHUNK, dtype=jnp.float32)
    T = lax.linalg.triangular_solve(eye + L, jnp.broadcast_to(eye, L.shape),
                                    left_side=True, lower=True)
    eg = jnp.exp(g)
    u = jnp.einsum('bhncm,bhnmd->bhncd', T, v * beta[..., None])
    w = jnp.einsum('bhncm,bhnmd->bhncd', T, kb * eg[..., None])
    attn = jnp.einsum('bhncd,bhnmd->bhncm', q, k) * decay
    q_dec = q * eg[..., None]
    k_dec = k * jnp.exp(g[..., -1:] - g)[..., None]
    g_last = eg[..., -1]

    def step(state, xs):
        qd, kd, uc, wc, ac, gl = xs
        v_new = uc - jnp.einsum('bhck,bhkv->bhcv', wc, state)
        o = jnp.einsum('bhck,bhkv->bhcv', qd, state) + jnp.einsum('bhcm,bhmv->bhcv', ac, v_new)
        state = state * gl[..., None, None] + jnp.einsum('bhck,bhcv->bhkv', kd, v_new)
        return state, o

    xs = tuple(jnp.moveaxis(t, 2, 0) for t in (q_dec, k_dec, u, w, attn, g_last))
    state0 = jnp.zeros((Bn, H, dk, dv), jnp.float32)
    _, o = lax.scan(step, state0, xs)
    o = jnp.moveaxis(o, 0, 2).reshape(Bn, H, S, dv)
    return jnp.moveaxis(o, 1, 2)


def _deltanet_branch(qa, ka, va, za, aa, ba, conv_w, a_log, dt_bias, norm_w):
    Bn, S, _ = qa.shape
    f32 = jnp.float32
    qkv = jax.nn.silu(_centred_dwconv(jnp.concatenate([qa, ka, va], axis=-1), conv_w))
    q, k, v = jnp.split(qkv.astype(f32), 3, axis=-1)
    shp = (Bn, S, A_HEADS, A_HEAD_DIM)
    q = _l2norm(q.reshape(shp)) * (A_HEAD_DIM ** -0.5)
    k = _l2norm(k.reshape(shp))
    v = v.reshape(shp)
    a = aa.astype(f32).reshape(Bn, S, 2, A_HEADS)
    b = ba.astype(f32).reshape(Bn, S, 2, A_HEADS)
    g = -jnp.exp(a_log.astype(f32)) * jax.nn.softplus(a + dt_bias.astype(f32))
    beta = jax.nn.sigmoid(b)
    o_fwd = _chunk_gated_delta(q, k, v, g[:, :, 0], beta[:, :, 0])
    rev = lambda t: jnp.flip(t, axis=1)
    o_bwd = rev(_chunk_gated_delta(rev(q), rev(k), rev(v), rev(g[:, :, 1]), rev(beta[:, :, 1])))
    o = _rms_norm(o_fwd + o_bwd, norm_w) * jax.nn.silu(za.astype(f32).reshape(shp))
    return o.reshape(Bn, S, A_WIDTH).astype(qa.dtype)


def _strided_window_attention(q, k, v, window, dilation):
    Bn, S, H, hd = q.shape
    R = window // (2 * dilation)
    L = S // dilation
    nb = -(-L // R)
    Lp = nb * R

    def residue_major(t):
        t = t.reshape(Bn, L, dilation, H, hd).transpose(0, 2, 3, 1, 4)
        return jnp.pad(t, ((0, 0), (0, 0), (0, 0), (0, Lp - L), (0, 0)))

    def neighbours(t):
        tp = jnp.pad(t, ((0, 0), (0, 0), (0, 0), (R, R), (0, 0)))
        tp = tp.reshape(Bn, dilation, H, nb + 2, R, hd)
        return jnp.concatenate([tp[:, :, :, :-2], tp[:, :, :, 1:-1], tp[:, :, :, 2:]], axis=4)

    qb = residue_major(q).reshape(Bn, dilation, H, nb, R, hd)
    kn = neighbours(residue_major(k))
    vn = neighbours(residue_major(v))
    qpos = jnp.arange(nb)[:, None] * R + jnp.arange(R)[None, :]
    kpos = (jnp.arange(nb)[:, None] - 1) * R + jnp.arange(3 * R)[None, :]
    rel = kpos[:, None, :] - qpos[:, :, None]
    valid = (jnp.abs(rel) <= R) & (kpos[:, None, :] >= 0) & (kpos[:, None, :] < L)
    s = jnp.einsum('bdhnqc,bdhnkc->bdhnqk', qb, kn) * (hd ** -0.5)
    s = jnp.where(valid, s, NEG_INF)
    m = jnp.max(s, axis=-1, keepdims=True)
    p = jnp.exp(s - m)
    den = jnp.sum(p, axis=-1)
    o = jnp.einsum('bdhnqk,bdhnkc->bdhnqc', p, vn) / den[..., None]
    lse = m[..., 0] + jnp.log(den)
    o = o.reshape(Bn, dilation, H, Lp, hd)[:, :, :, :L].transpose(0, 3, 1, 2, 4).reshape(Bn, S, H, hd)
    lse = lse.reshape(Bn, dilation, H, Lp)[..., :L].transpose(0, 3, 1, 2).reshape(Bn, S, H)
    return o, lse


def _dilated_branch(qb, kb, vb, zb):
    Bn, S, _ = qb.shape
    f32 = jnp.float32
    pos = jnp.arange(S)
    flat = (Bn, S, N_GROUPS * B_HEADS, B_HEAD_DIM)
    shp = (Bn, S, N_GROUPS, B_HEADS, B_HEAD_DIM)
    q = _rope(qb.astype(f32).reshape(flat), pos).reshape(shp)
    k = _rope(kb.astype(f32).reshape(flat), pos).reshape(shp)
    v = vb.astype(f32).reshape(shp)
    outs, lses = [], []
    for gi, (window, dilation) in enumerate(B_GROUPS):
        o, lse = _strided_window_attention(q[:, :, gi], k[:, :, gi], v[:, :, gi], window, dilation)
        outs.append(o)
        lses.append(lse)
    o = jnp.stack(outs, axis=2)
    wts = jax.nn.softmax(jnp.stack(lses, axis=2), axis=2)
    o = jnp.sum(wts[..., None] * o, axis=2)
    o = o * jax.nn.silu(zb.astype(f32).reshape(Bn, S, B_HEADS, B_HEAD_DIM))
    return o.reshape(Bn, S, B_GROUP_WIDTH).astype(qb.dtype)


def _encoder_layer(x, c, w_ada, b_ada, g_pre, g_post, w_in, conv_a, a_log, dt_bias,
                   norm_a, w_up_a, w_up_b, w_out):
    mod = jax.nn.silu(c) @ w_ada + b_ada
    shift, scale, gate = jnp.split(mod[:, None, :], 3, axis=-1)
    h = _rms_norm(x, g_pre) * (1 + scale) + shift
    proj = h @ w_in
    qa, ka, va, za, aa, ba, qb, kb, vb, zb, gates = jnp.split(proj, SPLIT_POINTS, axis=-1)
    ya = _deltanet_branch(qa, ka, va, za, aa, ba, conv_a, a_log, dt_bias, norm_a)
    yb = _dilated_branch(qb, kb, vb, zb)
    gate_a, gate_b = jnp.split(jax.nn.sigmoid(gates), 2, axis=-1)
    merged = gate_a * (ya @ w_up_a) + gate_b * (yb @ w_up_b)
    out = merged @ w_out
    return x + gate * _rms_norm(out, g_post)


def setup_inputs(seed: int = 0) -> dict:
    key = jax.random.key(seed)
    ks = jax.random.split(key, 16)
    f32 = jnp.float32

    def nrm(k, shape, std):
        return std * jax.random.normal(k, shape, f32)

    dt = jnp.exp(jax.random.uniform(ks[9], (DEPTH, 2, A_HEADS), f32,
                                    math.log(DT_MIN), math.log(DT_MAX)))
    return {
        'x_prompt': nrm(ks[0], (BATCH, SEQ, D_MODEL), 1.0),
        'x_sample': nrm(ks[1], (DEC_BATCH, DEC_SEQ, D_MODEL), 1.0),
        'c_prompt': nrm(ks[2], (BATCH, D_MODEL), 1.0),
        'c_sample': nrm(ks[3], (DEC_BATCH, D_MODEL), 1.0),
        'w_ada': nrm(ks[4], (DEPTH, D_MODEL, 3 * D_MODEL), ADA_INIT * D_MODEL ** -0.5),
        'b_ada': nrm(ks[5], (DEPTH, 3 * D_MODEL), 0.01),
        'norm_pre': 1.0 + nrm(ks[6], (DEPTH, D_MODEL), 0.02),
        'norm_post': 1.0 + nrm(ks[7], (DEPTH, D_MODEL), 0.02),
        'w_in': nrm(ks[8], (DEPTH, D_MODEL, N_IN), D_MODEL ** -0.5),
        'conv_a': nrm(ks[10], (DEPTH, CONV_K, 3 * A_WIDTH), CONV_K ** -0.5),
        'a_log': jnp.log(jax.random.uniform(ks[11], (DEPTH, 2, A_HEADS), f32, 1.0, 16.0)),
        'dt_bias': dt + jnp.log(-jnp.expm1(-dt)),
        'norm_a': 1.0 + nrm(ks[12], (DEPTH, A_HEAD_DIM), 0.02),
        'w_up_a': nrm(ks[13], (DEPTH, A_WIDTH, D_MODEL), A_WIDTH ** -0.5),
        'w_up_b': nrm(ks[14], (DEPTH, B_GROUP_WIDTH, D_MODEL), B_GROUP_WIDTH ** -0.5),
        'w_out': nrm(ks[15], (DEPTH, D_MODEL, D_MODEL), D_MODEL ** -0.5),
    }


def reference(x_prompt, x_sample, c_prompt, c_sample, w_ada, b_ada, norm_pre, norm_post,
              w_in, conv_a, a_log, dt_bias, norm_a, w_up_a, w_up_b, w_out):
    def trunk(x, c):
        for l in range(DEPTH):
            x = _encoder_layer(x, c, w_ada[l], b_ada[l], norm_pre[l], norm_post[l], w_in[l],
                               conv_a[l], a_log[l], dt_bias[l], norm_a[l], w_up_a[l],
                               w_up_b[l], w_out[l])
        return x

    y_prompt = trunk(x_prompt, c_prompt)
    y_sample = trunk(x_sample, c_sample)
    return (y_prompt, y_sample)
```

```python
import functools
import math

import jax
import jax.numpy as jnp
from jax import lax
from jax.experimental import pallas as pl
from jax.experimental.pallas import tpu as pltpu

F32 = jnp.float32
BF16 = jnp.bfloat16

LANE = 128
V7X_VMEM_BYTES = 64 * 1024 * 1024
VMEM_LIMIT_BYTES = V7X_VMEM_BYTES - 8 * 1024 * 1024

EPS = 1e-6
L2_EPS = 1e-6
ROPE_THETA = 10000.0
NEG_INF = -1e30
CONV_K = 5
CONV_HALO = 8
B_GROUPS = ((128, 1), (512, 4), (2048, 16))
GDN_CHUNK = 128
ATTN_TQ = 128


def _params(*sem):
    return pltpu.CompilerParams(dimension_semantics=sem, vmem_limit_bytes=VMEM_LIMIT_BYTES)


def _largest_tile(n, candidates):
    for c in candidates:
        if n % c == 0:
            return c
    raise ValueError(f"no tile in {candidates} divides {n}")


def _dot(a, b):
    return jnp.dot(a, b, preferred_element_type=F32)


def _dot_nt(a, b):
    return lax.dot_general(a, b, (((1,), (1,)), ((), ())), preferred_element_type=F32)


def _silu(x):
    return x * jax.nn.sigmoid(x)


def _mod_kernel(c_ref, w_ref, b_ref, o_ref):
    c = c_ref[...]
    o_ref[0] = _dot(_silu(c).astype(BF16), w_ref[0].astype(BF16)) + b_ref[0]


def _modulation(c_all, w_ada, b_ada):
    depth, d, n = w_ada.shape
    r = c_all.shape[0]
    tn = _largest_tile(n, (512, 256, 128))
    return pl.pallas_call(
        _mod_kernel,
        grid=(depth, n // tn),
        in_specs=[
            pl.BlockSpec((r, d), lambda l, j: (0, 0)),
            pl.BlockSpec((1, d, tn), lambda l, j: (l, 0, j)),
            pl.BlockSpec((1, 1, tn), lambda l, j: (l, 0, j)),
        ],
        out_specs=pl.BlockSpec((1, r, tn), lambda l, j: (l, 0, j)),
        out_shape=jax.ShapeDtypeStruct((depth, r, n), F32),
        compiler_params=_params("parallel", "parallel"),
        name="adaln_mod",
    )(c_all, w_ada, b_ada.reshape(depth, 1, n))


def _modulated_norm(x, gain, scale, shift):
    y = x * lax.rsqrt(jnp.mean(x * x, axis=-1, keepdims=True) + EPS)
    return (y * gain) * (1.0 + scale) + shift


def _prenorm_kernel(x_ref, sh_ref, sc_ref, g_ref, o_ref):
    o_ref[0] = _modulated_norm(x_ref[0], g_ref[...], sc_ref[0], sh_ref[0]).astype(BF16)


def _prenorm(x, mod, gain):
    b, s, d = x.shape
    ts = _largest_tile(s, (256, 128, 64, 32, 16, 8))
    return pl.pallas_call(
        _prenorm_kernel,
        grid=(b, s // ts),
        in_specs=[
            pl.BlockSpec((1, ts, d), lambda bi, i: (bi, i, 0)),
            pl.BlockSpec((1, 1, d), lambda bi, i: (bi, 0, 0)),
            pl.BlockSpec((1, 1, d), lambda bi, i: (bi, 0, 1)),
            pl.BlockSpec((1, d), lambda bi, i: (0, 0)),
        ],
        out_specs=pl.BlockSpec((1, ts, d), lambda bi, i: (bi, i, 0)),
        out_shape=jax.ShapeDtypeStruct((b, s, d), BF16),
        compiler_params=_params("parallel", "parallel"),
        name="prenorm",
    )(x, mod, mod, gain.reshape(1, d))


def _inproj_kernel(h_ref, w_ref, cos_ref, sin_ref, o_ref, *, heads_per_tile, rope_lo, rope_hi):
    j = pl.program_id(2)
    acc = _dot(h_ref[0], w_ref[...])
    for k in range(heads_per_tile):
        slab = acc[:, k * LANE:(k + 1) * LANE]
        head = j * heads_per_tile + k
        is_rope = jnp.logical_and(head >= rope_lo, head < rope_hi)

        @pl.when(is_rope)
        def _():
            rot = slab * cos_ref[...] + pltpu.roll(slab, LANE // 2, 1) * sin_ref[...]
            o_ref[0, k] = rot.astype(BF16)

        @pl.when(jnp.logical_not(is_rope))
        def _():
            o_ref[0, k] = slab.astype(BF16)


def _inproj(h, w, cos_t, sin_t, rope_lo, rope_hi):
    b, s, d = h.shape
    n = w.shape[1]
    tm = _largest_tile(s, (1024, 512, 256, 128))
    tn = _largest_tile(n, (1024, 512, 256, 128))
    hpt = tn // LANE
    kern = functools.partial(_inproj_kernel, heads_per_tile=hpt, rope_lo=rope_lo, rope_hi=rope_hi)
    return pl.pallas_call(
        kern,
        grid=(b, s // tm, n // tn),
        in_specs=[
            pl.BlockSpec((1, tm, d), lambda bi, i, j: (bi, i, 0)),
            pl.BlockSpec((d, tn), lambda bi, i, j: (0, j)),
            pl.BlockSpec((tm, LANE), lambda bi, i, j: (i, 0)),
            pl.BlockSpec((tm, LANE), lambda bi, i, j: (i, 0)),
        ],
        out_specs=pl.BlockSpec((1, hpt, tm, LANE), lambda bi, i, j: (bi, j, i, 0)),
        out_shape=jax.ShapeDtypeStruct((b, n // LANE, s, LANE), BF16),
        compiler_params=_params("parallel", "parallel", "arbitrary"),
        name="in_proj",
    )(h, w, cos_t, sin_t)


def _mm_kernel(x_ref, w_ref, o_ref):
    o_ref[0] = _dot(x_ref[0], w_ref[...]).astype(o_ref.dtype)


def _matmul(x, w, out_dtype, name):
    b, s, k = x.shape
    n = w.shape[1]
    tm = _largest_tile(s, (1024, 512, 256, 128))
    tn = _largest_tile(n, (1024, 512, 256, 128))
    return pl.pallas_call(
        _mm_kernel,
        grid=(b, s // tm, n // tn),
        in_specs=[
            pl.BlockSpec((1, tm, k), lambda bi, i, j: (bi, i, 0)),
            pl.BlockSpec((k, tn), lambda bi, i, j: (0, j)),
        ],
        out_specs=pl.BlockSpec((1, tm, tn), lambda bi, i, j: (bi, i, j)),
        out_shape=jax.ShapeDtypeStruct((b, s, n), out_dtype),
        compiler_params=_params("parallel", "parallel", "arbitrary"),
        name=name,
    )(x, w)


def _tri_inverse(lmat, n):
    row = lax.broadcasted_iota(jnp.int32, (n, n), 0)
    col = lax.broadcasted_iota(jnp.int32, (n, n), 1)
    t = jnp.where(row == col, 1.0, 0.0) - jnp.where((row >> 1) == (col >> 1), lmat, 0.0)
    for lv in range(1, int(math.log2(n))):
        join = jnp.logical_and((row >> (lv + 1)) == (col >> (lv + 1)), (row >> lv) != (col >> lv))
        tb = t.astype(BF16)
        t = t - _dot(_dot(tb, jnp.where(join, lmat, 0.0).astype(BF16)).astype(BF16), tb)
    return t


def _gdn_kernel(alog_ref, dtb_ref, q_ref, k_ref, v_ref, z_ref, ab_ref, cq_ref, ck_ref, cv_ref,
                nw_ref, o_ref, stage, qn, kn, vn, us, ws, qds, kdt, att, gls, *, seq, chunk, heads):
    S, C, H = seq, chunk, heads
    nc = S // C
    R = C
    nr = S // R
    h = pl.program_id(1)
    lane = lax.broadcasted_iota(jnp.int32, (1, LANE), 1)
    row = lax.broadcasted_iota(jnp.int32, (C, C), 0)
    col = lax.broadcasted_iota(jnp.int32, (C, C), 1)

    zeros_halo = jnp.zeros((CONV_HALO, LANE), F32)
    stage[0:CONV_HALO, :] = zeros_halo
    stage[S + CONV_HALO:S + 2 * CONV_HALO, :] = zeros_halo

    def conv_norm(src_ref, cw_ref, dst_ref, norm_scale):
        def fill(r, carry):
            r0 = pl.multiple_of(r * R, R)
            stage[pl.ds(CONV_HALO + r0, R), :] = src_ref[0, 0, pl.ds(r0, R), :].astype(F32)
            return carry

        lax.fori_loop(0, nr, fill, 0)

        def body(r, carry):
            r0 = pl.multiple_of(r * R, R)
            blk = stage[pl.ds(r0, R + 2 * CONV_HALO), :]
            y = jnp.zeros((R, LANE), F32)
            for j in range(CONV_K):
                off = CONV_HALO - CONV_K // 2 + j
                y = y + cw_ref[j:j + 1, :] * blk[off:off + R, :]
            y = _silu(y)
            if norm_scale is not None:
                y = y * lax.rsqrt(jnp.sum(y * y, axis=-1, keepdims=True) + L2_EPS) * norm_scale
            dst_ref[pl.ds(r0, R), :] = y.astype(dst_ref.dtype)
            return carry

        lax.fori_loop(0, nr, body, 0)

    conv_norm(q_ref, cq_ref, qn, LANE ** -0.5)
    conv_norm(k_ref, ck_ref, kn, 1.0)
    conv_norm(v_ref, cv_ref, vn, None)

    for dirn in (0, 1):
        fwd = dirn == 0
        incl = (row >= col) if fwd else (row <= col)
        strict = (row > col) if fwd else (row < col)
        g_lane = dirn * H + h
        b_lane = 2 * H + dirn * H + h

        def intra(c, carry, fwd=fwd, incl=incl, strict=strict, g_lane=g_lane, b_lane=b_lane):
            c0 = pl.multiple_of(c * C, C)
            qc = qn[pl.ds(c0, C), :].astype(F32)
            kc = kn[pl.ds(c0, C), :].astype(F32)
            vc = vn[pl.ds(c0, C), :].astype(F32)
            ab = ab_ref[0, pl.ds(c0, C), :]
            gval = -jnp.exp(alog_ref[...]) * jax.nn.softplus(ab + dtb_ref[...])
            cum = jnp.dot(incl.astype(F32), gval, precision=lax.Precision.HIGHEST,
                          preferred_element_type=F32)
            gc = jnp.sum(jnp.where(lane == g_lane, cum, 0.0), axis=1, keepdims=True)
            beta = jnp.sum(jnp.where(lane == b_lane, jax.nn.sigmoid(ab), 0.0), axis=1, keepdims=True)
            gcb = jnp.broadcast_to(gc, (C, C))
            decay = jnp.exp(jnp.where(incl, gcb - gcb.T, NEG_INF))
            kb = kc * beta
            kcb = kc.astype(BF16)
            lmat = jnp.where(strict, _dot_nt(kb.astype(BF16), kcb) * decay, 0.0)
            tinv = _tri_inverse(lmat, C)
            eg = jnp.exp(gc)
            rhs = jnp.concatenate([vc * beta, kb * eg], axis=1)
            uw = _dot(tinv.astype(BF16), rhs.astype(BF16))
            amat = jnp.where(incl, _dot_nt(qc.astype(BF16), kcb) * decay, 0.0)
            g_last = gc[C - 1:C, :] if fwd else gc[0:1, :]
            kd = kc * jnp.exp(g_last - gc)
            us[pl.ds(c0, C), :] = uw[:, :LANE].astype(BF16)
            ws[pl.ds(c0, C), :] = uw[:, LANE:].astype(BF16)
            qds[pl.ds(c0, C), :] = (qc * eg).astype(BF16)
            kdt[pl.ds(pl.multiple_of(c * LANE, LANE), LANE), :] = kd.T.astype(BF16)
            att[pl.ds(c0, C), :] = amat.astype(BF16)
            gls[pl.ds(c, 1), :] = jnp.broadcast_to(jnp.exp(g_last), (1, LANE))
            return carry

        lax.fori_loop(0, nc, intra, 0)

        def scan(t, state, fwd=fwd):
            c = t if fwd else nc - 1 - t
            c0 = pl.multiple_of(c * C, C)
            sb = state.astype(BF16)
            v_new = us[pl.ds(c0, C), :].astype(F32) - _dot(ws[pl.ds(c0, C), :], sb)
            vb = v_new.astype(BF16)
            o = _dot(qds[pl.ds(c0, C), :], sb) + _dot(att[pl.ds(c0, C), :], vb)
            rows = pl.ds(CONV_HALO + c0, C)
            if fwd:
                stage[rows, :] = o
            else:
                stage[rows, :] = stage[rows, :] + o
            k_t = kdt[pl.ds(pl.multiple_of(c * LANE, LANE), LANE), :]
            return state * gls[pl.ds(c, 1), :] + _dot(k_t, vb)

        lax.fori_loop(0, nc, scan, jnp.zeros((LANE, LANE), F32))

    def finish(r, carry):
        r0 = pl.multiple_of(r * R, R)
        o = stage[pl.ds(CONV_HALO + r0, R), :]
        y = o * lax.rsqrt(jnp.mean(o * o, axis=-1, keepdims=True) + EPS) * nw_ref[...]
        z = z_ref[0, 0, pl.ds(r0, R), :].astype(F32)
        o_ref[0, pl.ds(r0, R), :] = (y * _silu(z)).astype(BF16)
        return carry

    lax.fori_loop(0, nr, finish, 0)


def _single_buffered(shape, index_map):
    return pl.BlockSpec(shape, index_map, pipeline_mode=pl.Buffered(1))


def _gdn(p, ab, conv_w, alog_row, dtb_row, norm_w, heads):
    b, _, s, _ = p.shape
    c = GDN_CHUNK
    assert s % c == 0
    hq, hk, hv, hz = 0, heads, 2 * heads, 3 * heads

    def head_spec(off):
        return _single_buffered((1, 1, s, LANE), lambda bi, hi, off=off: (bi, off + hi, 0, 0))

    def conv_spec(off):
        return pl.BlockSpec((CONV_K, LANE), lambda bi, hi, off=off: (0, off + hi))

    row_spec = pl.BlockSpec((1, LANE), lambda bi, hi: (0, 0))
    kern = functools.partial(_gdn_kernel, seq=s, chunk=c, heads=heads)
    return pl.pallas_call(
        kern,
        grid=(b, heads),
        in_specs=[
            row_spec, row_spec,
            head_spec(hq), head_spec(hk), head_spec(hv), head_spec(hz),
            _single_buffered((1, s, LANE), lambda bi, hi: (bi, 0, 0)),
            conv_spec(hq), conv_spec(hk), conv_spec(hv),
            row_spec,
        ],
        out_specs=pl.BlockSpec((1, s, LANE), lambda bi, hi: (bi, 0, hi)),
        out_shape=jax.ShapeDtypeStruct((b, s, heads * LANE), BF16),
        scratch_shapes=[
            pltpu.VMEM((s + 2 * CONV_HALO, LANE), F32),
            pltpu.VMEM((s, LANE), BF16),
            pltpu.VMEM((s, LANE), BF16),
            pltpu.VMEM((s, LANE), BF16),
            pltpu.VMEM((s, LANE), BF16),
            pltpu.VMEM((s, LANE), BF16),
            pltpu.VMEM((s, LANE), BF16),
            pltpu.VMEM((s // c * LANE, c), BF16),
            pltpu.VMEM((s, c), BF16),
            pltpu.VMEM((max(s // c, 8), LANE), F32),
        ],
        compiler_params=_params("parallel", "arbitrary"),
        name="gated_deltanet",
    )(alog_row, dtb_row, p, p, p, p, ab, conv_w, conv_w, conv_w, norm_w)


def _attn_kernel(q_ref, k_ref, v_ref, o_ref, lse_ref, *, length, reach):
    L, R, TQ = length, reach, ATTN_TQ
    W = TQ + 2 * R
    scale = LANE ** -0.5

    def body(i, carry):
        q0 = pl.multiple_of(i * TQ, TQ)
        k0 = pl.multiple_of(jnp.clip(q0 - R, 0, L - W), R)
        q = q_ref[0, 0, pl.ds(q0, TQ), :]
        k = k_ref[0, 0, pl.ds(k0, W), :]
        v = v_ref[0, 0, pl.ds(k0, W), :]
        s = _dot_nt(q, k) * scale
        qpos = q0 + lax.broadcasted_iota(jnp.int32, (TQ, W), 0)
        kpos = k0 + lax.broadcasted_iota(jnp.int32, (TQ, W), 1)
        s = jnp.where(jnp.abs(kpos - qpos) <= R, s, NEG_INF)
        m = jnp.max(s, axis=1, keepdims=True)
        p = jnp.exp(s - m)
        den = jnp.sum(p, axis=1, keepdims=True)
        o = _dot(p.astype(BF16), v) / den
        o_ref[0, 0, pl.ds(q0, TQ), :] = o.astype(BF16)
        lse_ref[0, 0, pl.ds(q0, TQ), :] = jnp.broadcast_to(m + jnp.log(den), (TQ, LANE))
        return carry

    lax.fori_loop(0, L // TQ, body, 0)


def _window_attention(p, hq, hk, hv, heads, window, dilation):
    b, nh, s, _ = p.shape
    reach = window // (2 * dilation)
    length = s // dilation
    assert length % ATTN_TQ == 0 and length >= ATTN_TQ + 2 * reach
    pv = p.reshape(b, nh, length, dilation * LANE)

    def spec(off):
        return pl.BlockSpec((1, 1, length, LANE), lambda bi, hi, r, off=off: (bi, off + hi, 0, r))

    out_spec = pl.BlockSpec((1, 1, length, LANE), lambda bi, hi, r: (bi, hi, 0, r))
    kern = functools.partial(_attn_kernel, length=length, reach=reach)
    o, lse = pl.pallas_call(
        kern,
        grid=(b, heads, dilation),
        in_specs=[spec(hq), spec(hk), spec(hv)],
        out_specs=[out_spec, out_spec],
        out_shape=[jax.ShapeDtypeStruct((b, heads, length, dilation * LANE), BF16),
                   jax.ShapeDtypeStruct((b, heads, length, dilation * LANE), F32)],
        compiler_params=_params("parallel", "parallel", "parallel"),
        name=f"window_attention_d{dilation}",
    )(pv, pv, pv)
    return o.reshape(b, heads, s, LANE), lse.reshape(b, heads, s, LANE)


def _combine_kernel(o0, o1, o2, l0, l1, l2, z_ref, y_ref, *, heads):
    for h in range(heads):
        lses = [l[0, h] for l in (l0, l1, l2)]
        m = jnp.maximum(jnp.maximum(lses[0], lses[1]), lses[2])
        wts = [jnp.exp(l - m) for l in lses]
        den = wts[0] + wts[1] + wts[2]
        acc = sum(w * o[0, h].astype(F32) for w, o in zip(wts, (o0, o1, o2)))
        z = z_ref[0, h].astype(F32)
        y_ref[0, :, h * LANE:(h + 1) * LANE] = (acc / den * _silu(z)).astype(BF16)


def _combine(outs, lses, p, hz, heads):
    b, _, s, _ = p.shape
    ts = _largest_tile(s, (256, 128))
    assert hz % heads == 0
    spec = pl.BlockSpec((1, heads, ts, LANE), lambda bi, i: (bi, 0, i, 0))
    return pl.pallas_call(
        functools.partial(_combine_kernel, heads=heads),
        grid=(b, s // ts),
        in_specs=[spec] * 6 + [pl.BlockSpec((1, heads, ts, LANE), lambda bi, i: (bi, hz // heads, i, 0))],
        out_specs=pl.BlockSpec((1, ts, heads * LANE), lambda bi, i: (bi, i, 0)),
        out_shape=jax.ShapeDtypeStruct((b, s, heads * LANE), BF16),
        compiler_params=_params("parallel", "parallel"),
        name="group_combine",
    )(*outs, *lses, p)


def _merge_kernel(ya_ref, yb_ref, wa_ref, wb_ref, ga_ref, gb_ref, o_ref, *, heads_per_tile):
    acc_a = _dot(ya_ref[0], wa_ref[...])
    acc_b = _dot(yb_ref[0], wb_ref[...])
    for k in range(heads_per_tile):
        sl = slice(k * LANE, (k + 1) * LANE)
        ga = jax.nn.sigmoid(ga_ref[0, k].astype(F32))
        gb = jax.nn.sigmoid(gb_ref[0, k].astype(F32))
        o_ref[0, :, sl] = (ga * acc_a[:, sl] + gb * acc_b[:, sl]).astype(BF16)


def _merge(ya, yb, wa, wb, p, hga, hgb):
    b, s, ka = ya.shape
    kb = yb.shape[2]
    d = wa.shape[1]
    tm = _largest_tile(s, (1024, 512, 256, 128))
    tn = _largest_tile(math.gcd(d, hga * LANE, hgb * LANE), (1024, 512, 256, 128))
    hpt = tn // LANE
    return pl.pallas_call(
        functools.partial(_merge_kernel, heads_per_tile=hpt),
        grid=(b, s // tm, d // tn),
        in_specs=[
            pl.BlockSpec((1, tm, ka), lambda bi, i, j: (bi, i, 0)),
            pl.BlockSpec((1, tm, kb), lambda bi, i, j: (bi, i, 0)),
            pl.BlockSpec((ka, tn), lambda bi, i, j: (0, j)),
            pl.BlockSpec((kb, tn), lambda bi, i, j: (0, j)),
            pl.BlockSpec((1, hpt, tm, LANE), lambda bi, i, j: (bi, hga // hpt + j, i, 0)),
            pl.BlockSpec((1, hpt, tm, LANE), lambda bi, i, j: (bi, hgb // hpt + j, i, 0)),
        ],
        out_specs=pl.BlockSpec((1, tm, tn), lambda bi, i, j: (bi, i, j)),
        out_shape=jax.ShapeDtypeStruct((b, s, d), BF16),
        compiler_params=_params("parallel", "parallel", "arbitrary"),
        name="branch_merge",
    )(ya, yb, wa, wb, p, p)


def _final_kernel(*refs, with_next):
    if with_next:
        x_ref, out_ref, gate_ref, gpost_ref, sh_ref, sc_ref, gpre_ref, y_ref, h_ref = refs
    else:
        x_ref, out_ref, gate_ref, gpost_ref, y_ref = refs
    out = out_ref[0]
    normed = out * lax.rsqrt(jnp.mean(out * out, axis=-1, keepdims=True) + EPS) * gpost_ref[...]
    y = x_ref[0] + gate_ref[0] * normed
    y_ref[0] = y
    if with_next:
        h_ref[0] = _modulated_norm(y, gpre_ref[...], sc_ref[0], sh_ref[0]).astype(BF16)


def _finalize(x, out, mod, g_post, next_mod, next_gain):
    b, s, d = x.shape
    ts = _largest_tile(s, (256, 128, 64, 32, 16, 8))
    with_next = next_mod is not None
    tile = pl.BlockSpec((1, ts, d), lambda bi, i: (bi, i, 0))
    row = pl.BlockSpec((1, d), lambda bi, i: (0, 0))

    def mod_spec(part):
        return pl.BlockSpec((1, 1, d), lambda bi, i, part=part: (bi, 0, part))

    in_specs = [tile, tile, mod_spec(2), row]
    args = [x, out, mod, g_post.reshape(1, d)]
    out_specs = [tile]
    out_shape = [jax.ShapeDtypeStruct((b, s, d), F32)]
    if with_next:
        in_specs += [mod_spec(0), mod_spec(1), row]
        args += [next_mod, next_mod, next_gain.reshape(1, d)]
        out_specs.append(tile)
        out_shape.append(jax.ShapeDtypeStruct((b, s, d), BF16))
    res = pl.pallas_call(
        functools.partial(_final_kernel, with_next=with_next),
        grid=(b, s // ts),
        in_specs=in_specs,
        out_specs=out_specs,
        out_shape=out_shape,
        compiler_params=_params("parallel", "parallel"),
        name="postnorm_residual",
    )(*args)
    return (res[0], res[1]) if with_next else (res[0], None)


def _rope_tables(s):
    half = LANE // 2
    inv = ROPE_THETA ** (-jnp.arange(half, dtype=F32) / half)
    ang = jnp.arange(s, dtype=F32)[:, None] * inv[None, :]
    cos, sin = jnp.cos(ang), jnp.sin(ang)
    return jnp.concatenate([cos, cos], axis=1), jnp.concatenate([-sin, sin], axis=1)


def _pad_row(v):
    v = v.reshape(1, -1).astype(F32)
    return jnp.pad(v, ((0, 0), (0, LANE - v.shape[1])))


def kernel(x_prompt, x_sample, c_prompt, c_sample, w_ada, b_ada, norm_pre, norm_post, w_in, conv_a,
           a_log, dt_bias, norm_a, w_up_a, w_up_b, w_out):
    depth, d, _ = w_in.shape
    ah = d // 256
    bh = d // 512
    aw = ah * LANE
    n_ab = 4 * ah
    assert n_ab <= LANE
    bqkv = len(B_GROUPS) * bh * LANE
    bgw = bh * LANE
    hq_b = 4 * ah
    hk_b = hq_b + 3 * bh
    hv_b = hk_b + 3 * bh
    hz_b = hv_b + 3 * bh
    hg_a = hz_b + bh
    hg_b = hg_a + d // LANE
    c0 = 4 * aw
    c1 = c0 + n_ab

    bp, bs = c_prompt.shape[0], c_sample.shape[0]
    c_all = jnp.concatenate([c_prompt, c_sample], axis=0)
    pad_rows = -c_all.shape[0] % 8
    c_all = jnp.pad(c_all, ((0, pad_rows), (0, 0)))
    mod_all = _modulation(c_all, w_ada, b_ada)

    w_main = [jnp.concatenate([w_in[l, :, :c0], w_in[l, :, c1:]], axis=1).astype(BF16) for l in range(depth)]
    w_ab = [jnp.pad(w_in[l, :, c0:c1], ((0, 0), (0, LANE - n_ab))).astype(BF16) for l in range(depth)]
    wa = [w_up_a[l].astype(BF16) for l in range(depth)]
    wb = [w_up_b[l].astype(BF16) for l in range(depth)]
    wo = [w_out[l].astype(BF16) for l in range(depth)]

    def trunk(x, row0, nb):
        s = x.shape[1]
        cos_t, sin_t = _rope_tables(s)
        mods = [mod_all[l, row0:row0 + nb].reshape(nb, 1, 3 * d) for l in range(depth)]
        h = _prenorm(x, mods[0], norm_pre[0])
        for l in range(depth):
            p = _inproj(h, w_main[l], cos_t, sin_t, hq_b, hv_b)
            ab = _matmul(h, w_ab[l], F32, "in_proj_gates")
            ya = _gdn(p, ab, conv_a[l], _pad_row(a_log[l]), _pad_row(dt_bias[l]),
                      norm_a[l].reshape(1, LANE), ah)
            outs, lses = [], []
            for gi, (window, dilation) in enumerate(B_GROUPS):
                o, lse = _window_attention(p, hq_b + gi * bh, hk_b + gi * bh, hv_b + gi * bh, bh,
                                           window, dilation)
                outs.append(o)
                lses.append(lse)
            yb = _combine(outs, lses, p, hz_b, bh)
            merged = _merge(ya, yb, wa[l], wb[l], p, hg_a, hg_b)
            out = _matmul(merged, wo[l], F32, "out_proj")
            last = l == depth - 1
            x, h = _finalize(x, out, mods[l], norm_post[l],
                             None if last else mods[l + 1], None if last else norm_pre[l + 1])
        return x

    y_prompt = trunk(x_prompt, 0, bp)
    y_sample = trunk(x_sample, bp, bs)
    return (y_prompt, y_sample)
```

```python
import functools
import math

import jax
import jax.numpy as jnp
from jax import lax
from jax.experimental import pallas as pl
from jax.experimental.pallas import tpu as pltpu

F32 = jnp.float32
BF16 = jnp.bfloat16

LANE = 128
BF16_SUBLANES = 16
V7X_VMEM_BYTES = 64 * 1024 * 1024
VMEM_LIMIT_BYTES = V7X_VMEM_BYTES - 8 * 1024 * 1024

EPS = 1e-6
L2_EPS = 1e-6
ROPE_THETA = 10000.0
NEG_INF = -1e30
CONV_K = 5
B_GROUPS = ((128, 1), (512, 4), (2048, 16))
GDN_CHUNK = 128
GDN_CHUNKS_PER_STEP = 4
ATTN_TQ = 128
ATTN_BLOCKS_PER_STEP = 2
PERM_ROWS = 256


def _params(*sem):
    return pltpu.CompilerParams(dimension_semantics=sem, vmem_limit_bytes=VMEM_LIMIT_BYTES)


def _largest_tile(n, candidates):
    for c in candidates:
        if n % c == 0:
            return c
    raise ValueError(f"no tile in {candidates} divides {n}")


def _dot(a, b):
    return jnp.dot(a, b, preferred_element_type=F32)


def _dot_exact(a, b):
    return jnp.dot(a, b, precision=lax.Precision.HIGHEST, preferred_element_type=F32)


def _dot_nt(a, b):
    return lax.dot_general(a, b, (((1,), (1,)), ((), ())), preferred_element_type=F32)


def _silu(x):
    return x * jax.nn.sigmoid(x)


def _single_buffered(shape, index_map):
    return pl.BlockSpec(shape, index_map, pipeline_mode=pl.Buffered(1))


def _mod_kernel(c_ref, w_ref, b_ref, o_ref):
    c = c_ref[...]
    o_ref[0] = _dot(_silu(c).astype(BF16), w_ref[0].astype(BF16)) + b_ref[0]


def _modulation(c_all, w_ada, b_ada):
    depth, d, n = w_ada.shape
    r = c_all.shape[0]
    tn = _largest_tile(n, (512, 256, 128))
    return pl.pallas_call(
        _mod_kernel,
        grid=(depth, n // tn),
        in_specs=[
            pl.BlockSpec((r, d), lambda l, j: (0, 0)),
            pl.BlockSpec((1, d, tn), lambda l, j: (l, 0, j)),
            pl.BlockSpec((1, 1, tn), lambda l, j: (l, 0, j)),
        ],
        out_specs=pl.BlockSpec((1, r, tn), lambda l, j: (l, 0, j)),
        out_shape=jax.ShapeDtypeStruct((depth, r, n), F32),
        compiler_params=_params("parallel", "parallel"),
        name="adaln_mod",
    )(c_all, w_ada, b_ada.reshape(depth, 1, n))


def _modulated_norm(x, gain, scale, shift):
    y = x * lax.rsqrt(jnp.mean(x * x, axis=-1, keepdims=True) + EPS)
    return (y * gain) * (1.0 + scale) + shift


def _prenorm_kernel(x_ref, sh_ref, sc_ref, g_ref, o_ref):
    o_ref[0] = _modulated_norm(x_ref[0], g_ref[...], sc_ref[0], sh_ref[0]).astype(BF16)


def _prenorm(x, mod, gain):
    b, s, d = x.shape
    ts = _largest_tile(s, (256, 128, 64, 32, 16, 8))
    return pl.pallas_call(
        _prenorm_kernel,
        grid=(b, s // ts),
        in_specs=[
            pl.BlockSpec((1, ts, d), lambda bi, i: (bi, i, 0)),
            pl.BlockSpec((1, 1, d), lambda bi, i: (bi, 0, 0)),
            pl.BlockSpec((1, 1, d), lambda bi, i: (bi, 0, 1)),
            pl.BlockSpec((1, d), lambda bi, i: (0, 0)),
        ],
        out_specs=pl.BlockSpec((1, ts, d), lambda bi, i: (bi, i, 0)),
        out_shape=jax.ShapeDtypeStruct((b, s, d), BF16),
        compiler_params=_params("parallel", "parallel"),
        name="prenorm",
    )(x, mod, mod, gain.reshape(1, d))


def _residue_permutation(dilation):
    i = jnp.arange(PERM_ROWS)
    src = (i % (PERM_ROWS // dilation)) * dilation + i // (PERM_ROWS // dilation)
    return (src[:, None] == jnp.arange(PERM_ROWS)[None, :]).astype(BF16)


def _inproj_kernel(h_ref, w_ref, cos_ref, sin_ref, *rest, heads_per_tile, hq_b, hv_b, hz_b, bh, dilations):
    perm_refs, o_ref = rest[:-1], rest[-1]
    j = pl.program_id(2)
    tm = h_ref.shape[1]
    acc = _dot(h_ref[0], w_ref[...])
    perm_of = {}
    for d in dilations:
        if d > 1:
            perm_of[d] = perm_refs[len(perm_of)]
    for k in range(heads_per_tile):
        slab = acc[:, k * LANE:(k + 1) * LANE]
        head = j * heads_per_tile + k
        in_b = jnp.logical_and(head >= hq_b, head < hz_b)
        group = jnp.where(in_b, ((head - hq_b) // bh) % len(dilations), 0)
        is_rope = jnp.logical_and(head >= hq_b, head < hv_b)
        for rope in (False, True):
            for g, d in enumerate(dilations):

                rope_matches = is_rope if rope else jnp.logical_not(is_rope)

                @pl.when(jnp.logical_and(rope_matches, group == g))
                def _(rope=rope, d=d, k=k, slab=slab):
                    val = slab
                    if rope:
                        val = slab * cos_ref[...] + pltpu.roll(slab, LANE // 2, 1) * sin_ref[...]
                    vb = val.astype(BF16)
                    if d == 1:
                        o_ref[0, k] = vb
                    else:
                        for s in range(tm // PERM_ROWS):
                            rows = slice(s * PERM_ROWS, (s + 1) * PERM_ROWS)
                            o_ref[0, k, rows, :] = _dot(perm_of[d][...], vb[rows, :]).astype(BF16)


def _inproj(h, w, cos_t, sin_t, perms, hq_b, hv_b, hz_b, bh):
    b, s, d = h.shape
    n = w.shape[1]
    tm = _largest_tile(s, (1024, 512, 256))
    tn = _largest_tile(n, (1024, 512, 256, 128))
    hpt = tn // LANE
    dilations = tuple(dil for _, dil in B_GROUPS)
    kern = functools.partial(_inproj_kernel, heads_per_tile=hpt, hq_b=hq_b, hv_b=hv_b, hz_b=hz_b, bh=bh,
                             dilations=dilations)
    perm_spec = pl.BlockSpec((PERM_ROWS, PERM_ROWS), lambda bi, i, j: (0, 0))
    return pl.pallas_call(
        kern,
        grid=(b, s // tm, n // tn),
        in_specs=[
            pl.BlockSpec((1, tm, d), lambda bi, i, j: (bi, i, 0)),
            pl.BlockSpec((d, tn), lambda bi, i, j: (0, j)),
            pl.BlockSpec((tm, LANE), lambda bi, i, j: (i, 0)),
            pl.BlockSpec((tm, LANE), lambda bi, i, j: (i, 0)),
        ] + [perm_spec] * len(perms),
        out_specs=pl.BlockSpec((1, hpt, tm, LANE), lambda bi, i, j: (bi, j, i, 0)),
        out_shape=jax.ShapeDtypeStruct((b, n // LANE, s, LANE), BF16),
        compiler_params=_params("parallel", "parallel", "arbitrary"),
        name="in_proj",
    )(h, w, cos_t, sin_t, *perms)


def _mm_kernel(x_ref, w_ref, o_ref):
    o_ref[0] = _dot(x_ref[0], w_ref[...]).astype(o_ref.dtype)


def _matmul(x, w, out_dtype, name):
    b, s, k = x.shape
    n = w.shape[1]
    tm = _largest_tile(s, (1024, 512, 256, 128))
    tn = _largest_tile(n, (1024, 512, 256, 128))
    return pl.pallas_call(
        _mm_kernel,
        grid=(b, s // tm, n // tn),
        in_specs=[
            pl.BlockSpec((1, tm, k), lambda bi, i, j: (bi, i, 0)),
            pl.BlockSpec((k, tn), lambda bi, i, j: (0, j)),
        ],
        out_specs=pl.BlockSpec((1, tm, tn), lambda bi, i, j: (bi, i, j)),
        out_shape=jax.ShapeDtypeStruct((b, s, n), out_dtype),
        compiler_params=_params("parallel", "parallel", "arbitrary"),
        name=name,
    )(x, w)


def _gate_kernel(ab_ref, alog_ref, dtb_ref, o_ref, *, chunk, heads, tiles):
    C, H = chunk, heads
    row = lax.broadcasted_iota(jnp.int32, (C, C), 0)
    col = lax.broadcasted_iota(jnp.int32, (C, C), 1)
    lane = lax.broadcasted_iota(jnp.int32, (1, LANE), 1)
    lower = (row >= col).astype(F32)
    upper = (row <= col).astype(F32)
    for t in range(tiles):
        rows = slice(t * C, (t + 1) * C)
        ab = ab_ref[0, rows, :]
        gval = -jnp.exp(alog_ref[...]) * jax.nn.softplus(ab + dtb_ref[...])
        prefix = _dot_exact(lower, gval)
        suffix = _dot_exact(upper, gval)
        o_ref[0, rows, :] = jnp.where(lane < H, prefix, jnp.where(lane < 2 * H, suffix, jax.nn.sigmoid(ab)))


def _gates(ab, alog_row, dtb_row, heads):
    b, s, _ = ab.shape
    c = GDN_CHUNK
    tiles = 2 if s % (2 * c) == 0 else 1
    ts = tiles * c
    row_spec = pl.BlockSpec((1, LANE), lambda bi, i: (0, 0))
    tile = pl.BlockSpec((1, ts, LANE), lambda bi, i: (bi, i, 0))
    return pl.pallas_call(
        functools.partial(_gate_kernel, chunk=c, heads=heads, tiles=tiles),
        grid=(b, s // ts),
        in_specs=[tile, row_spec, row_spec],
        out_specs=tile,
        out_shape=jax.ShapeDtypeStruct((b, s, LANE), F32),
        compiler_params=_params("parallel", "parallel"),
        name="deltanet_gates",
    )(ab, alog_row, dtb_row)


def _tri_inverse_many(lmats, n):
    row = lax.broadcasted_iota(jnp.int32, (n, n), 0)
    col = lax.broadcasted_iota(jnp.int32, (n, n), 1)
    eye = jnp.where(row == col, 1.0, 0.0)
    pair = (row >> 1) == (col >> 1)
    ts = [eye - jnp.where(pair, lm, 0.0) for lm in lmats]
    for lv in range(1, int(math.log2(n))):
        join = jnp.logical_and((row >> (lv + 1)) == (col >> (lv + 1)), (row >> lv) != (col >> lv))
        tbs = [t.astype(BF16) for t in ts]
        mids = [_dot(tb, jnp.where(join, lm, 0.0).astype(BF16)) for tb, lm in zip(tbs, lmats)]
        ts = [t - _dot(mid.astype(BF16), tb) for t, mid, tb in zip(ts, mids, tbs)]
    return ts


def _gdn_kernel(q_ref, k_ref, v_ref, z_ref, gate_ref, cq_ref, ck_ref, cv_ref, nw_ref, o_ref,
                oacc, qn, kn, vn, us, ws, qds, kdt, att, gls, *, seq, chunk, heads):
    S, C, H = seq, chunk, heads
    nc = S // C
    G = GDN_CHUNKS_PER_STEP
    R = C
    nr = S // R
    halo = BF16_SUBLANES
    h = pl.program_id(1)
    lane = lax.broadcasted_iota(jnp.int32, (1, LANE), 1)
    row = lax.broadcasted_iota(jnp.int32, (C, C), 0)
    col = lax.broadcasted_iota(jnp.int32, (C, C), 1)

    def conv_tile(src_ref, cw_ref, dst_ref, norm_scale, r):
        r0 = pl.multiple_of(r * R, R)
        prev0 = pl.multiple_of(jnp.maximum(r0 - halo, 0), halo)
        next0 = pl.multiple_of(jnp.minimum(r0 + R, S - halo), halo)
        prev = src_ref[0, 0, pl.ds(prev0, halo), :].astype(F32)
        nxt = src_ref[0, 0, pl.ds(next0, halo), :].astype(F32)
        prev = jnp.where(r > 0, prev, 0.0)
        nxt = jnp.where(r < nr - 1, nxt, 0.0)
        blk = jnp.concatenate([prev, src_ref[0, 0, pl.ds(r0, R), :].astype(F32), nxt], axis=0)
        y = jnp.zeros((R, LANE), F32)
        for j in range(CONV_K):
            off = halo - CONV_K // 2 + j
            y = y + cw_ref[j:j + 1, :] * blk[off:off + R, :]
        y = _silu(y)
        if norm_scale is not None:
            y = y * (lax.rsqrt(jnp.sum(y * y, axis=-1, keepdims=True) + L2_EPS) * norm_scale)
        dst_ref[pl.ds(r0, R), :] = y.astype(dst_ref.dtype)

    def conv_body(r, carry):
        conv_tile(q_ref, cq_ref, qn, LANE ** -0.5, r)
        conv_tile(k_ref, ck_ref, kn, 1.0, r)
        conv_tile(v_ref, cv_ref, vn, None, r)
        oacc[pl.ds(pl.multiple_of(r * R, R), R), :] = jnp.zeros((R, LANE), F32)
        return carry

    lax.fori_loop(0, nr, conv_body, 0)

    masks = [((row >= col), (row > col)), ((row <= col), (row < col))]

    def intra_body(i, carry):
        loaded = []
        for g in range(G):
            c = i * G + g
            c0 = pl.multiple_of(c * C, C)
            loaded.append((c, c0, qn[pl.ds(c0, C), :], kn[pl.ds(c0, C), :],
                           vn[pl.ds(c0, C), :].astype(F32), gate_ref[0, pl.ds(c0, C), :]))
        kks = [_dot_nt(kb, kb) for (_, _, _, kb, _, _) in loaded]
        qks = [_dot_nt(qb, kb) for (_, _, qb, kb, _, _) in loaded]
        chains = []
        for (c, c0, qb, kb, vc, gt), kk, qk in zip(loaded, kks, qks):
            for dirn in (0, 1):
                incl, strict = masks[dirn]
                gc = jnp.sum(jnp.where(lane == dirn * H + h, gt, 0.0), axis=1, keepdims=True)
                beta = jnp.sum(jnp.where(lane == (2 + dirn) * H + h, gt, 0.0), axis=1, keepdims=True)
                gcb = jnp.broadcast_to(gc, (C, C))
                decay = jnp.exp(jnp.where(incl, gcb - gcb.T, NEG_INF))
                lmat = jnp.where(strict, (beta * kk) * decay, 0.0)
                amat = jnp.where(incl, qk * decay, 0.0)
                chains.append((c, c0, dirn, qb, kb, vc, gc, beta, lmat, amat))
        tinvs = _tri_inverse_many([ch[8] for ch in chains], C)
        rhss = []
        for (c, c0, dirn, qb, kb, vc, gc, beta, lmat, amat) in chains:
            rhs = jnp.concatenate([vc * beta, kb.astype(F32) * (beta * jnp.exp(gc))], axis=1)
            rhss.append(rhs.astype(BF16))
        uws = [_dot(tinv.astype(BF16), rhs) for tinv, rhs in zip(tinvs, rhss)]
        for (c, c0, dirn, qb, kb, vc, gc, beta, lmat, amat), uw in zip(chains, uws):
            g_last = gc[C - 1:C, :] if dirn == 0 else gc[0:1, :]
            kd = kb.astype(F32) * jnp.exp(g_last - gc)
            us[dirn, pl.ds(c0, C), :] = uw[:, :LANE].astype(BF16)
            ws[dirn, pl.ds(c0, C), :] = uw[:, LANE:].astype(BF16)
            qds[dirn, pl.ds(c0, C), :] = (qb.astype(F32) * jnp.exp(gc)).astype(BF16)
            kdt[dirn, pl.ds(pl.multiple_of(c * LANE, LANE), LANE), :] = kd.T.astype(BF16)
            att[dirn, pl.ds(c0, C), :] = amat.astype(BF16)
            gls[dirn, pl.ds(c, 1), :] = jnp.broadcast_to(jnp.exp(g_last), (1, LANE))
        return carry

    lax.fori_loop(0, nc // G, intra_body, 0)

    def scan_body(t, states):
        cs = (t, nc - 1 - t)
        c0s = [pl.multiple_of(c * C, C) for c in cs]
        sbs = [st.astype(BF16) for st in states]
        wss = [_dot(ws[d, pl.ds(c0s[d], C), :], sbs[d]) for d in (0, 1)]
        qss = [_dot(qds[d, pl.ds(c0s[d], C), :], sbs[d]) for d in (0, 1)]
        vbs = [(us[d, pl.ds(c0s[d], C), :].astype(F32) - wss[d]).astype(BF16) for d in (0, 1)]
        avs = [_dot(att[d, pl.ds(c0s[d], C), :], vbs[d]) for d in (0, 1)]
        kvs = [_dot(kdt[d, pl.ds(pl.multiple_of(cs[d] * LANE, LANE), LANE), :], vbs[d]) for d in (0, 1)]
        for d in (0, 1):
            rows = pl.ds(c0s[d], C)
            oacc[rows, :] = oacc[rows, :] + (qss[d] + avs[d])
        return tuple(states[d] * gls[d, pl.ds(cs[d], 1), :] + kvs[d] for d in (0, 1))

    zero_state = jnp.zeros((LANE, LANE), F32)
    lax.fori_loop(0, nc, scan_body, (zero_state, zero_state))

    def finish_tile(r):
        r0 = pl.multiple_of(r * R, R)
        o = oacc[pl.ds(r0, R), :]
        y = o * lax.rsqrt(jnp.mean(o * o, axis=-1, keepdims=True) + EPS) * nw_ref[...]
        z = z_ref[0, 0, pl.ds(r0, R), :].astype(F32)
        o_ref[0, pl.ds(r0, R), :] = (y * _silu(z)).astype(BF16)

    def finish_body(i, carry):
        finish_tile(2 * i)
        finish_tile(2 * i + 1)
        return carry

    lax.fori_loop(0, nr // 2, finish_body, 0)


def _gdn(p, gates, conv_w, norm_w, heads):
    b, _, s, _ = p.shape
    c = GDN_CHUNK
    nc = s // c
    assert s % (2 * c) == 0 and nc % GDN_CHUNKS_PER_STEP == 0
    hq, hk, hv, hz = 0, heads, 2 * heads, 3 * heads

    def head_spec(off):
        return _single_buffered((1, 1, s, LANE), lambda bi, hi, off=off: (bi, off + hi, 0, 0))

    def conv_spec(off):
        return pl.BlockSpec((CONV_K, LANE), lambda bi, hi, off=off: (0, off + hi))

    kern = functools.partial(_gdn_kernel, seq=s, chunk=c, heads=heads)
    return pl.pallas_call(
        kern,
        grid=(b, heads),
        in_specs=[
            head_spec(hq), head_spec(hk), head_spec(hv), head_spec(hz),
            _single_buffered((1, s, LANE), lambda bi, hi: (bi, 0, 0)),
            conv_spec(hq), conv_spec(hk), conv_spec(hv),
            pl.BlockSpec((1, LANE), lambda bi, hi: (0, 0)),
        ],
        out_specs=pl.BlockSpec((1, s, LANE), lambda bi, hi: (bi, 0, hi)),
        out_shape=jax.ShapeDtypeStruct((b, s, heads * LANE), BF16),
        scratch_shapes=[
            pltpu.VMEM((s, LANE), F32),
            pltpu.VMEM((s, LANE), BF16),
            pltpu.VMEM((s, LANE), BF16),
            pltpu.VMEM((s, LANE), BF16),
            pltpu.VMEM((2, s, LANE), BF16),
            pltpu.VMEM((2, s, LANE), BF16),
            pltpu.VMEM((2, s, LANE), BF16),
            pltpu.VMEM((2, nc * LANE, c), BF16),
            pltpu.VMEM((2, s, c), BF16),
            pltpu.VMEM((2, max(nc, 8), LANE), F32),
        ],
        compiler_params=_params("parallel", "arbitrary"),
        name="gated_deltanet",
    )(p, p, p, p, gates, conv_w, conv_w, conv_w, norm_w)


def _attend_many(blocks, reach):
    scores = [_dot_nt(q, k) * (LANE ** -0.5) for (q, k, _, _, _) in blocks]
    probs, stats = [], []
    for s, (q, k, _, q0, k0) in zip(scores, blocks):
        tq, w = q.shape[0], k.shape[0]
        qpos = q0 + lax.broadcasted_iota(jnp.int32, (tq, w), 0)
        kpos = k0 + lax.broadcasted_iota(jnp.int32, (tq, w), 1)
        s = jnp.where(jnp.abs(kpos - qpos) <= reach, s, NEG_INF)
        m = jnp.max(s, axis=1, keepdims=True)
        p = jnp.exp(s - m)
        den = jnp.sum(p, axis=1, keepdims=True)
        probs.append(p.astype(BF16))
        stats.append((m, den))
    outs = [_dot(p, v) for p, (_, _, v, _, _) in zip(probs, blocks)]
    return [((o / den).astype(BF16), jnp.broadcast_to(m + jnp.log(den), (o.shape[0], LANE)))
            for o, (m, den) in zip(outs, stats)]


def _attn_kernel(q_ref, k_ref, v_ref, o_ref, lse_ref, *scratch, length, reach, rows):
    L, R, TQ = length, reach, ATTN_TQ
    W = TQ + 2 * R
    U = ATTN_BLOCKS_PER_STEP
    if rows:
        qs, ks, vs = scratch

        def gather(g, carry):
            dst = pl.ds(pl.multiple_of(g * rows, rows), rows)
            qs[dst, :] = q_ref[0, 0, g, 0]
            ks[dst, :] = k_ref[0, 0, g, 0]
            vs[dst, :] = v_ref[0, 0, g, 0]
            return carry

        lax.fori_loop(0, L // rows, gather, 0)

    def body(step, carry):
        blocks = []
        for u in range(U):
            i = step * U + u
            q0 = pl.multiple_of(i * TQ, TQ)
            k0 = pl.multiple_of(jnp.clip(q0 - R, 0, L - W), R)
            if rows:
                blocks.append((qs[pl.ds(q0, TQ), :], ks[pl.ds(k0, W), :], vs[pl.ds(k0, W), :], q0, k0))
            else:
                blocks.append((q_ref[0, 0, pl.ds(q0, TQ), :], k_ref[0, 0, pl.ds(k0, W), :],
                               v_ref[0, 0, pl.ds(k0, W), :], q0, k0))
        for u, (o, lse) in enumerate(_attend_many(blocks, R)):
            i = step * U + u
            if rows:
                for j in range(TQ // rows):
                    part = slice(j * rows, (j + 1) * rows)
                    o_ref[0, 0, i * (TQ // rows) + j, 0] = o[part, :]
                    lse_ref[0, 0, i * (TQ // rows) + j, 0] = lse[part, :]
            else:
                q0 = pl.multiple_of(i * TQ, TQ)
                o_ref[0, 0, pl.ds(q0, TQ), :] = o
                lse_ref[0, 0, pl.ds(q0, TQ), :] = lse
        return carry

    lax.fori_loop(0, L // TQ // U, body, 0)


def _window_attention(p, hq, hk, hv, heads, window, dilation):
    b, nh, s, _ = p.shape
    reach = window // (2 * dilation)
    length = s // dilation
    assert length % (ATTN_TQ * ATTN_BLOCKS_PER_STEP) == 0 and length >= ATTN_TQ + 2 * reach
    if dilation == 1:
        rows = 0
        pv = p
        out_dims = (b, heads, s, LANE)
        block = (1, 1, s, LANE)

        def spec(off):
            return pl.BlockSpec(block, lambda bi, hi, r, off=off: (bi, off + hi, 0, 0))

        scratch = []
    else:
        rows = PERM_ROWS // dilation
        assert rows % BF16_SUBLANES == 0 and ATTN_TQ % rows == 0 and s % PERM_ROWS == 0
        ng = s // PERM_ROWS
        pv = p.reshape(b, nh, ng, dilation, rows, LANE)
        out_dims = (b, heads, ng, dilation, rows, LANE)
        block = (1, 1, ng, 1, rows, LANE)

        def spec(off):
            return pl.BlockSpec(block, lambda bi, hi, r, off=off: (bi, off + hi, 0, r, 0, 0))

        scratch = [pltpu.VMEM((length, LANE), BF16)] * 3
    kern = functools.partial(_attn_kernel, length=length, reach=reach, rows=rows)
    o, lse = pl.pallas_call(
        kern,
        grid=(b, heads, dilation),
        in_specs=[spec(hq), spec(hk), spec(hv)],
        out_specs=[spec(0), spec(0)],
        out_shape=[jax.ShapeDtypeStruct(out_dims, BF16), jax.ShapeDtypeStruct(out_dims, F32)],
        scratch_shapes=scratch,
        compiler_params=_params("parallel", "parallel", "parallel"),
        name=f"window_attention_d{dilation}",
    )(pv, pv, pv)
    return o.reshape(b, heads, s, LANE), lse.reshape(b, heads, s, LANE)


def _combine_kernel(*refs, heads, dilations):
    ng = len(dilations)
    o_refs, l_refs = refs[:ng], refs[ng:2 * ng]
    z_ref = refs[2 * ng]
    unperm_refs = refs[2 * ng + 1:-1]
    y_ref = refs[-1]
    unperm_of = {}
    for d in dilations:
        if d > 1:
            unperm_of[d] = unperm_refs[len(unperm_of)]
    for h in range(heads):
        outs, lses = [], []
        for g, d in enumerate(dilations):
            o, lse = o_refs[g][0, h], l_refs[g][0, h]
            if d > 1:
                o = _dot(unperm_of[d][...], o)
                lse = _dot_exact(unperm_of[d][...].astype(F32), lse)
            else:
                o = o.astype(F32)
            outs.append(o)
            lses.append(lse)
        m = functools.reduce(jnp.maximum, lses)
        wts = [jnp.exp(l - m) for l in lses]
        den = functools.reduce(lambda a, c: a + c, wts)
        acc = functools.reduce(lambda a, c: a + c, [w * o for w, o in zip(wts, outs)])
        z = z_ref[0, h].astype(F32)
        y_ref[0, :, h * LANE:(h + 1) * LANE] = (acc / den * _silu(z)).astype(BF16)


def _combine(outs, lses, unperms, p, hz, heads):
    b, _, s, _ = p.shape
    ts = PERM_ROWS
    assert hz % heads == 0 and s % ts == 0
    dilations = tuple(dil for _, dil in B_GROUPS)
    spec = pl.BlockSpec((1, heads, ts, LANE), lambda bi, i: (bi, 0, i, 0))
    n = len(outs)
    return pl.pallas_call(
        functools.partial(_combine_kernel, heads=heads, dilations=dilations),
        grid=(b, s // ts),
        in_specs=[spec] * (2 * n)
        + [pl.BlockSpec((1, heads, ts, LANE), lambda bi, i: (bi, hz // heads, i, 0))]
        + [pl.BlockSpec((PERM_ROWS, PERM_ROWS), lambda bi, i: (0, 0))] * len(unperms),
        out_specs=pl.BlockSpec((1, ts, heads * LANE), lambda bi, i: (bi, i, 0)),
        out_shape=jax.ShapeDtypeStruct((b, s, heads * LANE), BF16),
        compiler_params=_params("parallel", "parallel"),
        name="group_combine",
    )(*outs, *lses, p, *unperms)


def _merge_kernel(ya_ref, yb_ref, wa_ref, wb_ref, ga_ref, gb_ref, o_ref, *, heads_per_tile):
    acc_a = _dot(ya_ref[0], wa_ref[...])
    acc_b = _dot(yb_ref[0], wb_ref[...])
    for k in range(heads_per_tile):
        sl = slice(k * LANE, (k + 1) * LANE)
        ga = jax.nn.sigmoid(ga_ref[0, k].astype(F32))
        gb = jax.nn.sigmoid(gb_ref[0, k].astype(F32))
        o_ref[0, :, sl] = (ga * acc_a[:, sl] + gb * acc_b[:, sl]).astype(BF16)


def _merge(ya, yb, wa, wb, p, hga, hgb):
    b, s, ka = ya.shape
    kb = yb.shape[2]
    d = wa.shape[1]
    tm = _largest_tile(s, (1024, 512, 256, 128))
    tn = _largest_tile(math.gcd(d, hga * LANE, hgb * LANE), (1024, 512, 256, 128))
    hpt = tn // LANE
    return pl.pallas_call(
        functools.partial(_merge_kernel, heads_per_tile=hpt),
        grid=(b, s // tm, d // tn),
        in_specs=[
            pl.BlockSpec((1, tm, ka), lambda bi, i, j: (bi, i, 0)),
            pl.BlockSpec((1, tm, kb), lambda bi, i, j: (bi, i, 0)),
            pl.BlockSpec((ka, tn), lambda bi, i, j: (0, j)),
            pl.BlockSpec((kb, tn), lambda bi, i, j: (0, j)),
            pl.BlockSpec((1, hpt, tm, LANE), lambda bi, i, j: (bi, hga // hpt + j, i, 0)),
            pl.BlockSpec((1, hpt, tm, LANE), lambda bi, i, j: (bi, hgb // hpt + j, i, 0)),
        ],
        out_specs=pl.BlockSpec((1, tm, tn), lambda bi, i, j: (bi, i, j)),
        out_shape=jax.ShapeDtypeStruct((b, s, d), BF16),
        compiler_params=_params("parallel", "parallel", "arbitrary"),
        name="branch_merge",
    )(ya, yb, wa, wb, p, p)


def _final_kernel(*refs, with_next):
    if with_next:
        x_ref, out_ref, gate_ref, gpost_ref, sh_ref, sc_ref, gpre_ref, y_ref, h_ref = refs
    else:
        x_ref, out_ref, gate_ref, gpost_ref, y_ref = refs
    out = out_ref[0]
    normed = out * lax.rsqrt(jnp.mean(out * out, axis=-1, keepdims=True) + EPS) * gpost_ref[...]
    y = x_ref[0] + gate_ref[0] * normed
    y_ref[0] = y
    if with_next:
        h_ref[0] = _modulated_norm(y, gpre_ref[...], sc_ref[0], sh_ref[0]).astype(BF16)


def _finalize(x, out, mod, g_post, next_mod, next_gain):
    b, s, d = x.shape
    ts = _largest_tile(s, (256, 128, 64, 32, 16, 8))
    with_next = next_mod is not None
    tile = pl.BlockSpec((1, ts, d), lambda bi, i: (bi, i, 0))
    row = pl.BlockSpec((1, d), lambda bi, i: (0, 0))

    def mod_spec(part):
        return pl.BlockSpec((1, 1, d), lambda bi, i, part=part: (bi, 0, part))

    in_specs = [tile, tile, mod_spec(2), row]
    args = [x, out, mod, g_post.reshape(1, d)]
    out_specs = [tile]
    out_shape = [jax.ShapeDtypeStruct((b, s, d), F32)]
    if with_next:
        in_specs += [mod_spec(0), mod_spec(1), row]
        args += [next_mod, next_mod, next_gain.reshape(1, d)]
        out_specs.append(tile)
        out_shape.append(jax.ShapeDtypeStruct((b, s, d), BF16))
    res = pl.pallas_call(
        functools.partial(_final_kernel, with_next=with_next),
        grid=(b, s // ts),
        in_specs=in_specs,
        out_specs=out_specs,
        out_shape=out_shape,
        compiler_params=_params("parallel", "parallel"),
        name="postnorm_residual",
    )(*args)
    return (res[0], res[1]) if with_next else (res[0], None)


def _rope_tables(s):
    half = LANE // 2
    inv = ROPE_THETA ** (-jnp.arange(half, dtype=F32) / half)
    ang = jnp.arange(s, dtype=F32)[:, None] * inv[None, :]
    cos, sin = jnp.cos(ang), jnp.sin(ang)
    return jnp.concatenate([cos, cos], axis=1), jnp.concatenate([-sin, sin], axis=1)


def _pad_row(v):
    v = v.reshape(1, -1).astype(F32)
    return jnp.pad(v, ((0, 0), (0, LANE - v.shape[1])))


def kernel(x_prompt, x_sample, c_prompt, c_sample, w_ada, b_ada, norm_pre, norm_post, w_in, conv_a,
           a_log, dt_bias, norm_a, w_up_a, w_up_b, w_out):
    depth, d, _ = w_in.shape
    ah = d // 256
    bh = d // 512
    aw = ah * LANE
    n_ab = 4 * ah
    assert n_ab <= LANE
    hq_b = 4 * ah
    hk_b = hq_b + 3 * bh
    hv_b = hk_b + 3 * bh
    hz_b = hv_b + 3 * bh
    hg_a = hz_b + bh
    hg_b = hg_a + d // LANE
    c0 = 4 * aw
    c1 = c0 + n_ab

    bp, bs = c_prompt.shape[0], c_sample.shape[0]
    c_all = jnp.concatenate([c_prompt, c_sample], axis=0)
    c_all = jnp.pad(c_all, ((0, -c_all.shape[0] % 8), (0, 0)))
    mod_all = _modulation(c_all, w_ada, b_ada)

    w_main = [jnp.concatenate([w_in[l, :, :c0], w_in[l, :, c1:]], axis=1).astype(BF16) for l in range(depth)]
    w_ab = [jnp.pad(w_in[l, :, c0:c1], ((0, 0), (0, LANE - n_ab))).astype(BF16) for l in range(depth)]
    wa = [w_up_a[l].astype(BF16) for l in range(depth)]
    wb = [w_up_b[l].astype(BF16) for l in range(depth)]
    wo = [w_out[l].astype(BF16) for l in range(depth)]
    perms = [_residue_permutation(dil) for _, dil in B_GROUPS if dil > 1]
    unperms = [pm.T for pm in perms]

    def trunk(x, row0, nb):
        s = x.shape[1]
        cos_t, sin_t = _rope_tables(s)
        mods = [mod_all[l, row0:row0 + nb].reshape(nb, 1, 3 * d) for l in range(depth)]
        h = _prenorm(x, mods[0], norm_pre[0])
        for l in range(depth):
            p = _inproj(h, w_main[l], cos_t, sin_t, perms, hq_b, hv_b, hz_b, bh)
            ab = _matmul(h, w_ab[l], F32, "in_proj_gates")
            gates = _gates(ab, _pad_row(a_log[l]), _pad_row(dt_bias[l]), ah)
            ya = _gdn(p, gates, conv_a[l], norm_a[l].reshape(1, LANE), ah)
            outs, lses = [], []
            for gi, (window, dilation) in enumerate(B_GROUPS):
                o, lse = _window_attention(p, hq_b + gi * bh, hk_b + gi * bh, hv_b + gi * bh, bh,
                                           window, dilation)
                outs.append(o)
                lses.append(lse)
            yb = _combine(outs, lses, unperms, p, hz_b, bh)
            merged = _merge(ya, yb, wa[l], wb[l], p, hg_a, hg_b)
            out = _matmul(merged, wo[l], F32, "out_proj")
            last = l == depth - 1
            x, h = _finalize(x, out, mods[l], norm_post[l],
                             None if last else mods[l + 1], None if last else norm_pre[l + 1])
        return x

    y_prompt = trunk(x_prompt, 0, bp)
    y_sample = trunk(x_sample, bp, bs)
    return (y_prompt, y_sample)
```

```python
import functools
import math

import jax
import jax.numpy as jnp
from jax import lax
from jax.experimental import pallas as pl
from jax.experimental.pallas import tpu as pltpu

F32 = jnp.float32
BF16 = jnp.bfloat16

LANE = 128
BF16_SUBLANES = 16
V7X_VMEM_BYTES = 64 * 1024 * 1024
VMEM_LIMIT_BYTES = V7X_VMEM_BYTES - 8 * 1024 * 1024

EPS = 1e-6
L2_EPS = 1e-6
ROPE_THETA = 10000.0
NEG_INF = -1e30
CONV_K = 5
B_GROUPS = ((128, 1), (512, 4), (2048, 16))
GDN_CHUNK = 128
GDN_CHUNKS_PER_STEP = 4
ATTN_TQ = 128
ATTN_BLOCKS_PER_STEP = 4
PERM_ROWS = 256


def _params(*sem):
    return pltpu.CompilerParams(dimension_semantics=sem, vmem_limit_bytes=VMEM_LIMIT_BYTES)


def _largest_tile(n, candidates):
    for c in candidates:
        if n % c == 0:
            return c
    raise ValueError(f"no tile in {candidates} divides {n}")


def _dot(a, b):
    return jnp.dot(a, b, preferred_element_type=F32)


def _dot_exact(a, b):
    return jnp.dot(a, b, precision=lax.Precision.HIGHEST, preferred_element_type=F32)


def _dot_nt(a, b):
    return lax.dot_general(a, b, (((1,), (1,)), ((), ())), preferred_element_type=F32)


def _silu(x):
    return x * jax.nn.sigmoid(x)


def _single_buffered(shape, index_map):
    return pl.BlockSpec(shape, index_map, pipeline_mode=pl.Buffered(1))


def _mod_kernel(c_ref, w_ref, b_ref, o_ref):
    c = c_ref[...]
    o_ref[0] = _dot(_silu(c).astype(BF16), w_ref[0].astype(BF16)) + b_ref[0]


def _modulation(c_all, w_ada, b_ada):
    depth, d, n = w_ada.shape
    r = c_all.shape[0]
    tn = _largest_tile(n, (512, 256, 128))
    return pl.pallas_call(
        _mod_kernel,
        grid=(depth, n // tn),
        in_specs=[
            pl.BlockSpec((r, d), lambda l, j: (0, 0)),
            pl.BlockSpec((1, d, tn), lambda l, j: (l, 0, j)),
            pl.BlockSpec((1, 1, tn), lambda l, j: (l, 0, j)),
        ],
        out_specs=pl.BlockSpec((1, r, tn), lambda l, j: (l, 0, j)),
        out_shape=jax.ShapeDtypeStruct((depth, r, n), F32),
        compiler_params=_params("parallel", "parallel"),
        name="adaln_mod",
    )(c_all, w_ada, b_ada.reshape(depth, 1, n))


def _modulated_norm(x, gain, scale, shift):
    y = x * lax.rsqrt(jnp.mean(x * x, axis=-1, keepdims=True) + EPS)
    return (y * gain) * (1.0 + scale) + shift


def _prenorm_kernel(x_ref, sh_ref, sc_ref, g_ref, o_ref):
    o_ref[0] = _modulated_norm(x_ref[0], g_ref[...], sc_ref[0], sh_ref[0]).astype(BF16)


def _prenorm(x, mod, gain):
    b, s, d = x.shape
    ts = _largest_tile(s, (256, 128, 64, 32, 16, 8))
    return pl.pallas_call(
        _prenorm_kernel,
        grid=(b, s // ts),
        in_specs=[
            pl.BlockSpec((1, ts, d), lambda bi, i: (bi, i, 0)),
            pl.BlockSpec((1, 1, d), lambda bi, i: (bi, 0, 0)),
            pl.BlockSpec((1, 1, d), lambda bi, i: (bi, 0, 1)),
            pl.BlockSpec((1, d), lambda bi, i: (0, 0)),
        ],
        out_specs=pl.BlockSpec((1, ts, d), lambda bi, i: (bi, i, 0)),
        out_shape=jax.ShapeDtypeStruct((b, s, d), BF16),
        compiler_params=_params("parallel", "parallel"),
        name="prenorm",
    )(x, mod, mod, gain.reshape(1, d))


def _residue_permutation(dilation):
    i = jnp.arange(PERM_ROWS)
    src = (i % (PERM_ROWS // dilation)) * dilation + i // (PERM_ROWS // dilation)
    return (src[:, None] == jnp.arange(PERM_ROWS)[None, :]).astype(BF16)


def _inproj_kernel(h_ref, w_ref, cos_ref, sin_ref, *rest, heads_per_tile, hq_b, hv_b, hz_b, bh, dilations):
    perm_refs, o_ref = rest[:-1], rest[-1]
    tm = h_ref.shape[1]
    acc = _dot(h_ref[0], w_ref[...])
    perm_of = {}
    for d in dilations:
        if d > 1:
            perm_of[d] = perm_refs[len(perm_of)]
    head0 = pl.program_id(2) * heads_per_tile
    in_b = jnp.logical_and(head0 >= hq_b, head0 < hz_b)
    group = jnp.where(in_b, ((head0 - hq_b) // bh) % len(dilations), 0)
    is_rope = jnp.logical_and(head0 >= hq_b, head0 < hv_b)
    for rope in (False, True):
        rope_matches = is_rope if rope else jnp.logical_not(is_rope)
        for g, d in enumerate(dilations):

            @pl.when(jnp.logical_and(rope_matches, group == g))
            def _(rope=rope, d=d):
                slabs = []
                for k in range(heads_per_tile):
                    slab = acc[:, k * LANE:(k + 1) * LANE]
                    if rope:
                        slab = slab * cos_ref[...] + pltpu.roll(slab, LANE // 2, 1) * sin_ref[...]
                    slabs.append(slab.astype(BF16))
                if d == 1:
                    for k in range(heads_per_tile):
                        o_ref[0, k] = slabs[k]
                else:
                    tile = jnp.concatenate(slabs, axis=1)
                    for s in range(tm // PERM_ROWS):
                        rows = slice(s * PERM_ROWS, (s + 1) * PERM_ROWS)
                        moved = _dot(perm_of[d][...], tile[rows, :]).astype(BF16)
                        for k in range(heads_per_tile):
                            o_ref[0, k, rows, :] = moved[:, k * LANE:(k + 1) * LANE]


def _inproj(h, w, cos_t, sin_t, perms, hq_b, hv_b, hz_b, bh):
    b, s, d = h.shape
    n = w.shape[1]
    tm = _largest_tile(s, (1024, 512, 256))
    tn = _largest_tile(math.gcd(n, bh * LANE), (1024, 512, 256, 128))
    hpt = tn // LANE
    assert hq_b % hpt == 0
    dilations = tuple(dil for _, dil in B_GROUPS)
    kern = functools.partial(_inproj_kernel, heads_per_tile=hpt, hq_b=hq_b, hv_b=hv_b, hz_b=hz_b, bh=bh,
                             dilations=dilations)
    perm_spec = pl.BlockSpec((PERM_ROWS, PERM_ROWS), lambda bi, i, j: (0, 0))
    return pl.pallas_call(
        kern,
        grid=(b, s // tm, n // tn),
        in_specs=[
            pl.BlockSpec((1, tm, d), lambda bi, i, j: (bi, i, 0)),
            pl.BlockSpec((d, tn), lambda bi, i, j: (0, j)),
            pl.BlockSpec((tm, LANE), lambda bi, i, j: (i, 0)),
            pl.BlockSpec((tm, LANE), lambda bi, i, j: (i, 0)),
        ] + [perm_spec] * len(perms),
        out_specs=pl.BlockSpec((1, hpt, tm, LANE), lambda bi, i, j: (bi, j, i, 0)),
        out_shape=jax.ShapeDtypeStruct((b, n // LANE, s, LANE), BF16),
        compiler_params=_params("parallel", "parallel", "arbitrary"),
        name="in_proj",
    )(h, w, cos_t, sin_t, *perms)


def _mm_kernel(x_ref, w_ref, o_ref):
    o_ref[0] = _dot(x_ref[0], w_ref[...]).astype(o_ref.dtype)


def _matmul(x, w, out_dtype, name):
    b, s, k = x.shape
    n = w.shape[1]
    tm = _largest_tile(s, (1024, 512, 256, 128))
    tn = _largest_tile(n, (1024, 512, 256, 128))
    return pl.pallas_call(
        _mm_kernel,
        grid=(b, s // tm, n // tn),
        in_specs=[
            pl.BlockSpec((1, tm, k), lambda bi, i, j: (bi, i, 0)),
            pl.BlockSpec((k, tn), lambda bi, i, j: (0, j)),
        ],
        out_specs=pl.BlockSpec((1, tm, tn), lambda bi, i, j: (bi, i, j)),
        out_shape=jax.ShapeDtypeStruct((b, s, n), out_dtype),
        compiler_params=_params("parallel", "parallel", "arbitrary"),
        name=name,
    )(x, w)


def _gate_kernel(ab_ref, alog_ref, dtb_ref, o_ref, *, chunk, heads, tiles):
    C, H = chunk, heads
    row = lax.broadcasted_iota(jnp.int32, (C, C), 0)
    col = lax.broadcasted_iota(jnp.int32, (C, C), 1)
    lane = lax.broadcasted_iota(jnp.int32, (1, LANE), 1)
    lower = (row >= col).astype(F32)
    upper = (row <= col).astype(F32)
    for t in range(tiles):
        rows = slice(t * C, (t + 1) * C)
        ab = ab_ref[0, rows, :]
        gval = -jnp.exp(alog_ref[...]) * jax.nn.softplus(ab + dtb_ref[...])
        prefix = _dot_exact(lower, gval)
        suffix = _dot_exact(upper, gval)
        o_ref[0, rows, :] = jnp.where(lane < H, prefix, jnp.where(lane < 2 * H, suffix, jax.nn.sigmoid(ab)))


def _gates(ab, alog_row, dtb_row, heads):
    b, s, _ = ab.shape
    c = GDN_CHUNK
    tiles = 2 if s % (2 * c) == 0 else 1
    ts = tiles * c
    row_spec = pl.BlockSpec((1, LANE), lambda bi, i: (0, 0))
    tile = pl.BlockSpec((1, ts, LANE), lambda bi, i: (bi, i, 0))
    return pl.pallas_call(
        functools.partial(_gate_kernel, chunk=c, heads=heads, tiles=tiles),
        grid=(b, s // ts),
        in_specs=[tile, row_spec, row_spec],
        out_specs=tile,
        out_shape=jax.ShapeDtypeStruct((b, s, LANE), F32),
        compiler_params=_params("parallel", "parallel"),
        name="deltanet_gates",
    )(ab, alog_row, dtb_row)


def _tri_inverse_stages(lmats, n, result):
    row = lax.broadcasted_iota(jnp.int32, (n, n), 0)
    col = lax.broadcasted_iota(jnp.int32, (n, n), 1)
    eye = jnp.where(row == col, 1.0, 0.0)
    pair = (row >> 1) == (col >> 1)
    ts = [eye - jnp.where(pair, lm, 0.0) for lm in lmats]
    for lv in range(1, int(math.log2(n))):
        join = jnp.logical_and((row >> (lv + 1)) == (col >> (lv + 1)), (row >> lv) != (col >> lv))
        tbs = [t.astype(BF16) for t in ts]
        mids = [_dot(tb, jnp.where(join, lm, 0.0).astype(BF16)) for tb, lm in zip(tbs, lmats)]
        yield
        ts = [t - _dot(mid.astype(BF16), tb) for t, mid, tb in zip(ts, mids, tbs)]
        yield
    result.extend(ts)


def _emit_interleaved(generators):
    active = list(generators)
    while active:
        for gen in list(active):
            try:
                next(gen)
            except StopIteration:
                active.remove(gen)


def _gdn_kernel(q_ref, k_ref, v_ref, z_ref, gate_ref, cq_ref, ck_ref, cv_ref, nw_ref, o_ref,
                oacc, cstage, qn, kn, vn, us, ws, qds, kdt, att, gls, *, seq, chunk, heads):
    S, C, H = seq, chunk, heads
    nc = S // C
    G = GDN_CHUNKS_PER_STEP
    R = C
    nr = S // R
    halo = BF16_SUBLANES
    h = pl.program_id(1)
    lane = lax.broadcasted_iota(jnp.int32, (1, LANE), 1)
    row = lax.broadcasted_iota(jnp.int32, (C, C), 0)
    col = lax.broadcasted_iota(jnp.int32, (C, C), 1)

    def conv_tile(slot, src_ref, cw_ref, dst_ref, norm_scale, r):
        r0 = pl.multiple_of(r * R, R)
        prev0 = pl.multiple_of(jnp.maximum(r0 - halo, 0), halo)
        next0 = pl.multiple_of(jnp.minimum(r0 + R, S - halo), halo)
        prev = src_ref[0, 0, pl.ds(prev0, halo), :].astype(F32)
        nxt = src_ref[0, 0, pl.ds(next0, halo), :].astype(F32)
        cstage[slot, 0:halo, :] = jnp.where(r > 0, prev, 0.0)
        cstage[slot, halo:halo + R, :] = src_ref[0, 0, pl.ds(r0, R), :].astype(F32)
        cstage[slot, halo + R:, :] = jnp.where(r < nr - 1, nxt, 0.0)
        y = jnp.zeros((R, LANE), F32)
        for j in range(CONV_K):
            off = halo - CONV_K // 2 + j
            y = y + cw_ref[j:j + 1, :] * cstage[slot, off:off + R, :]
        y = _silu(y)
        if norm_scale is not None:
            y = y * (lax.rsqrt(jnp.sum(y * y, axis=-1, keepdims=True) + L2_EPS) * norm_scale)
        dst_ref[pl.ds(r0, R), :] = y.astype(dst_ref.dtype)

    def conv_body(r, carry):
        conv_tile(0, q_ref, cq_ref, qn, LANE ** -0.5, r)
        conv_tile(1, k_ref, ck_ref, kn, 1.0, r)
        conv_tile(2, v_ref, cv_ref, vn, None, r)
        oacc[pl.ds(pl.multiple_of(r * R, R), R), :] = jnp.zeros((R, LANE), F32)
        return carry

    lax.fori_loop(0, nr, conv_body, 0)

    ng = nc // G
    masks = [((row >= col), (row > col)), ((row <= col), (row < col))]

    def chain_chunks(i):
        return [(0, g, i * G + g) for g in range(G)] + [(1, g, nc - (i + 1) * G + g) for g in range(G)]

    def prepare_stages(i, slot):
        items = []
        for dirn, g, c in chain_chunks(i):
            c0 = pl.multiple_of(c * C, C)
            items.append((dirn, g, qn[pl.ds(c0, C), :], kn[pl.ds(c0, C), :],
                          vn[pl.ds(c0, C), :].astype(F32), gate_ref[0, pl.ds(c0, C), :]))
        kks = [_dot_nt(kb, kb) for (_, _, _, kb, _, _) in items]
        qks = [_dot_nt(qb, kb) for (_, _, qb, kb, _, _) in items]
        yield
        chains = []
        for (dirn, g, qb, kb, vc, gt), kk, qk in zip(items, kks, qks):
            incl, strict = masks[dirn]
            gc = jnp.sum(jnp.where(lane == dirn * H + h, gt, 0.0), axis=1, keepdims=True)
            beta = jnp.sum(jnp.where(lane == (2 + dirn) * H + h, gt, 0.0), axis=1, keepdims=True)
            gcb = jnp.broadcast_to(gc, (C, C))
            decay = jnp.exp(jnp.where(incl, gcb - gcb.T, NEG_INF))
            lmat = jnp.where(strict, (beta * kk) * decay, 0.0)
            amat = jnp.where(incl, qk * decay, 0.0)
            chains.append((dirn, g, qb, kb, vc, gc, beta, amat, lmat))
        tinvs = []
        yield from _tri_inverse_stages([ch[8] for ch in chains], C, tinvs)
        rhss = []
        for (dirn, g, qb, kb, vc, gc, beta, amat, lmat) in chains:
            rhs = jnp.concatenate([vc * beta, kb.astype(F32) * (beta * jnp.exp(gc))], axis=1)
            rhss.append(rhs.astype(BF16))
        uws = [_dot(tinv.astype(BF16), rhs) for tinv, rhs in zip(tinvs, rhss)]
        yield
        for (dirn, g, qb, kb, vc, gc, beta, amat, lmat), uw in zip(chains, uws):
            g_last = gc[C - 1:C, :] if dirn == 0 else gc[0:1, :]
            kd = kb.astype(F32) * jnp.exp(g_last - gc)
            rows = slice(g * C, (g + 1) * C)
            us[slot, dirn, rows, :] = uw[:, :LANE].astype(BF16)
            ws[slot, dirn, rows, :] = uw[:, LANE:].astype(BF16)
            qds[slot, dirn, rows, :] = (qb.astype(F32) * jnp.exp(gc)).astype(BF16)
            kdt[slot, dirn, g * LANE:(g + 1) * LANE, :] = kd.T.astype(BF16)
            att[slot, dirn, rows, :] = amat.astype(BF16)
            gls[slot, dirn, g:g + 1, :] = jnp.broadcast_to(jnp.exp(g_last), (1, LANE))

    def update_stages(i, slot, states, result):
        states = list(states)
        for step in range(G):
            gs = (step, G - 1 - step)
            cs = (i * G + gs[0], nc - (i + 1) * G + gs[1])
            rows = [slice(g * C, (g + 1) * C) for g in gs]
            sbs = [st.astype(BF16) for st in states]
            wss = [_dot(ws[slot, d, rows[d], :], sbs[d]) for d in (0, 1)]
            qss = [_dot(qds[slot, d, rows[d], :], sbs[d]) for d in (0, 1)]
            yield
            vbs = [(us[slot, d, rows[d], :].astype(F32) - wss[d]).astype(BF16) for d in (0, 1)]
            avs = [_dot(att[slot, d, rows[d], :], vbs[d]) for d in (0, 1)]
            kvs = [_dot(kdt[slot, d, gs[d] * LANE:(gs[d] + 1) * LANE, :], vbs[d]) for d in (0, 1)]
            yield
            for d in (0, 1):
                out_rows = pl.ds(pl.multiple_of(cs[d] * C, C), C)
                oacc[out_rows, :] = oacc[out_rows, :] + (qss[d] + avs[d])
            states = [states[d] * gls[slot, d, gs[d]:gs[d] + 1, :] + kvs[d] for d in (0, 1)]
        result.extend(states)

    _emit_interleaved([prepare_stages(0, 0)])

    def pipelined_body(i, states):
        new_states = []
        _emit_interleaved([prepare_stages(i, i % 2), update_stages(i - 1, (i - 1) % 2, states, new_states)])
        return tuple(new_states)

    zero_state = jnp.zeros((LANE, LANE), F32)
    states = lax.fori_loop(1, ng, pipelined_body, (zero_state, zero_state))
    _emit_interleaved([update_stages(ng - 1, (ng - 1) % 2, states, [])])

    def finish_tile(r):
        r0 = pl.multiple_of(r * R, R)
        o = oacc[pl.ds(r0, R), :]
        y = o * lax.rsqrt(jnp.mean(o * o, axis=-1, keepdims=True) + EPS) * nw_ref[...]
        z = z_ref[0, 0, pl.ds(r0, R), :].astype(F32)
        o_ref[0, pl.ds(r0, R), :] = (y * _silu(z)).astype(BF16)

    def finish_body(i, carry):
        finish_tile(2 * i)
        finish_tile(2 * i + 1)
        return carry

    lax.fori_loop(0, nr // 2, finish_body, 0)


def _gdn(p, gates, conv_w, norm_w, heads):
    b, _, s, _ = p.shape
    c = GDN_CHUNK
    nc = s // c
    g = GDN_CHUNKS_PER_STEP
    assert s % (2 * c) == 0 and nc % (2 * g) == 0
    hq, hk, hv, hz = 0, heads, 2 * heads, 3 * heads

    def head_spec(off):
        return _single_buffered((1, 1, s, LANE), lambda bi, hi, off=off: (bi, off + hi, 0, 0))

    def conv_spec(off):
        return pl.BlockSpec((CONV_K, LANE), lambda bi, hi, off=off: (0, off + hi))

    kern = functools.partial(_gdn_kernel, seq=s, chunk=c, heads=heads)
    return pl.pallas_call(
        kern,
        grid=(b, heads),
        in_specs=[
            head_spec(hq), head_spec(hk), head_spec(hv), head_spec(hz),
            _single_buffered((1, s, LANE), lambda bi, hi: (bi, 0, 0)),
            conv_spec(hq), conv_spec(hk), conv_spec(hv),
            pl.BlockSpec((1, LANE), lambda bi, hi: (0, 0)),
        ],
        out_specs=pl.BlockSpec((1, s, LANE), lambda bi, hi: (bi, 0, hi)),
        out_shape=jax.ShapeDtypeStruct((b, s, heads * LANE), BF16),
        scratch_shapes=[
            pltpu.VMEM((s, LANE), F32),
            pltpu.VMEM((3, c + 2 * BF16_SUBLANES, LANE), F32),
            pltpu.VMEM((s, LANE), BF16),
            pltpu.VMEM((s, LANE), BF16),
            pltpu.VMEM((s, LANE), BF16),
            pltpu.VMEM((2, 2, g * c, LANE), BF16),
            pltpu.VMEM((2, 2, g * c, LANE), BF16),
            pltpu.VMEM((2, 2, g * c, LANE), BF16),
            pltpu.VMEM((2, 2, g * LANE, c), BF16),
            pltpu.VMEM((2, 2, g * c, c), BF16),
            pltpu.VMEM((2, 2, max(g, 8), LANE), F32),
        ],
        compiler_params=_params("parallel", "arbitrary"),
        name="gated_deltanet",
    )(p, p, p, p, gates, conv_w, conv_w, conv_w, norm_w)


def _attend_many(blocks, reach):
    scores = [_dot_nt(q, k) * (LANE ** -0.5) for (q, k, _, _, _) in blocks]
    probs, stats = [], []
    for s, (q, k, _, q0, k0) in zip(scores, blocks):
        tq, w = q.shape[0], k.shape[0]
        qpos = q0 + lax.broadcasted_iota(jnp.int32, (tq, w), 0)
        kpos = k0 + lax.broadcasted_iota(jnp.int32, (tq, w), 1)
        s = jnp.where(jnp.abs(kpos - qpos) <= reach, s, NEG_INF)
        m = jnp.max(s, axis=1, keepdims=True)
        p = jnp.exp(s - m)
        den = jnp.sum(p, axis=1, keepdims=True)
        probs.append(p.astype(BF16))
        stats.append((m, den))
    outs = [_dot(p, v) for p, (_, _, v, _, _) in zip(probs, blocks)]
    return [((o / den).astype(BF16), jnp.broadcast_to(m + jnp.log(den), (o.shape[0], LANE)))
            for o, (m, den) in zip(outs, stats)]


def _attn_kernel(q_ref, k_ref, v_ref, o_ref, lse_ref, *scratch, length, reach, rows, dilation, unroll):
    L, R, TQ, U = length, reach, ATTN_TQ, unroll
    W = TQ + 2 * R

    def residue(r, carry):
        if rows:
            qs, ks, vs = scratch

            def gather(g, c):
                dst = pl.ds(pl.multiple_of(g * rows, rows), rows)
                qs[dst, :] = q_ref[0, 0, g, r]
                ks[dst, :] = k_ref[0, 0, g, r]
                vs[dst, :] = v_ref[0, 0, g, r]
                return c

            lax.fori_loop(0, L // rows, gather, 0)

        def body(step, c):
            blocks = []
            for u in range(U):
                i = step * U + u
                q0 = pl.multiple_of(i * TQ, TQ)
                k0 = pl.multiple_of(jnp.clip(q0 - R, 0, L - W), R)
                if rows:
                    blocks.append((qs[pl.ds(q0, TQ), :], ks[pl.ds(k0, W), :], vs[pl.ds(k0, W), :], q0, k0))
                else:
                    blocks.append((q_ref[0, 0, pl.ds(q0, TQ), :], k_ref[0, 0, pl.ds(k0, W), :],
                                   v_ref[0, 0, pl.ds(k0, W), :], q0, k0))
            for u, (o, lse) in enumerate(_attend_many(blocks, R)):
                i = step * U + u
                if rows:
                    for j in range(TQ // rows):
                        part = slice(j * rows, (j + 1) * rows)
                        o_ref[0, 0, i * (TQ // rows) + j, r] = o[part, :]
                        lse_ref[0, 0, i * (TQ // rows) + j, r] = lse[part, :]
                else:
                    q0 = pl.multiple_of(i * TQ, TQ)
                    o_ref[0, 0, pl.ds(q0, TQ), :] = o
                    lse_ref[0, 0, pl.ds(q0, TQ), :] = lse
            return c

        lax.fori_loop(0, L // TQ // U, body, 0)
        return carry

    if rows:
        lax.fori_loop(0, dilation, residue, 0)
    else:
        residue(0, 0)


def _window_attention(p, hq, hk, hv, heads, window, dilation):
    b, nh, s, _ = p.shape
    reach = window // (2 * dilation)
    length = s // dilation
    unroll = math.gcd(ATTN_BLOCKS_PER_STEP, length // ATTN_TQ)
    assert length % ATTN_TQ == 0 and length >= ATTN_TQ + 2 * reach
    if dilation == 1:
        rows = 0
        pv = p
        out_dims = (b, heads, s, LANE)
        block = (1, 1, s, LANE)
        tail = (0, 0)
        scratch = []
    else:
        rows = PERM_ROWS // dilation
        assert rows % BF16_SUBLANES == 0 and ATTN_TQ % rows == 0 and s % PERM_ROWS == 0
        ng = s // PERM_ROWS
        pv = p.reshape(b, nh, ng, dilation, rows, LANE)
        out_dims = (b, heads, ng, dilation, rows, LANE)
        block = (1, 1, ng, dilation, rows, LANE)
        tail = (0, 0, 0, 0)
        scratch = [pltpu.VMEM((length, LANE), BF16)] * 3

    def spec(off):
        return pl.BlockSpec(block, lambda bi, hi, off=off: (bi, off + hi) + tail)

    kern = functools.partial(_attn_kernel, length=length, reach=reach, rows=rows, dilation=dilation,
                             unroll=unroll)
    o, lse = pl.pallas_call(
        kern,
        grid=(b, heads),
        in_specs=[spec(hq), spec(hk), spec(hv)],
        out_specs=[spec(0), spec(0)],
        out_shape=[jax.ShapeDtypeStruct(out_dims, BF16), jax.ShapeDtypeStruct(out_dims, F32)],
        scratch_shapes=scratch,
        compiler_params=_params("parallel", "parallel"),
        name=f"window_attention_d{dilation}",
    )(pv, pv, pv)
    return o.reshape(b, heads, s, LANE), lse.reshape(b, heads, s, LANE)


def _select_rows_f32(sel, x):
    hi = x.astype(BF16)
    rest = x - hi.astype(F32)
    mid = rest.astype(BF16)
    lo = (rest - mid.astype(F32)).astype(BF16)
    return _dot(sel, hi) + _dot(sel, mid) + _dot(sel, lo)


def _combine_kernel(*refs, heads, dilations):
    ng = len(dilations)
    o_refs, l_refs = refs[:ng], refs[ng:2 * ng]
    z_ref = refs[2 * ng]
    unperm_refs = refs[2 * ng + 1:-1]
    y_ref = refs[-1]
    unperm_of = {}
    for d in dilations:
        if d > 1:
            unperm_of[d] = unperm_refs[len(unperm_of)]
    for h in range(heads):
        outs, lses = [], []
        for g, d in enumerate(dilations):
            o, lse = o_refs[g][0, h], l_refs[g][0, h]
            if d > 1:
                o = _dot(unperm_of[d][...], o)
                lse = _select_rows_f32(unperm_of[d][...], lse)
            else:
                o = o.astype(F32)
            outs.append(o)
            lses.append(lse)
        m = functools.reduce(jnp.maximum, lses)
        wts = [jnp.exp(l - m) for l in lses]
        den = functools.reduce(lambda a, c: a + c, wts)
        acc = functools.reduce(lambda a, c: a + c, [w * o for w, o in zip(wts, outs)])
        z = z_ref[0, h].astype(F32)
        y_ref[0, :, h * LANE:(h + 1) * LANE] = (acc / den * _silu(z)).astype(BF16)


def _combine(outs, lses, unperms, p, hz, heads):
    b, _, s, _ = p.shape
    ts = PERM_ROWS
    assert hz % heads == 0 and s % ts == 0
    dilations = tuple(dil for _, dil in B_GROUPS)
    spec = pl.BlockSpec((1, heads, ts, LANE), lambda bi, i: (bi, 0, i, 0))
    n = len(outs)
    return pl.pallas_call(
        functools.partial(_combine_kernel, heads=heads, dilations=dilations),
        grid=(b, s // ts),
        in_specs=[spec] * (2 * n)
        + [pl.BlockSpec((1, heads, ts, LANE), lambda bi, i: (bi, hz // heads, i, 0))]
        + [pl.BlockSpec((PERM_ROWS, PERM_ROWS), lambda bi, i: (0, 0))] * len(unperms),
        out_specs=pl.BlockSpec((1, ts, heads * LANE), lambda bi, i: (bi, i, 0)),
        out_shape=jax.ShapeDtypeStruct((b, s, heads * LANE), BF16),
        compiler_params=_params("parallel", "parallel"),
        name="group_combine",
    )(*outs, *lses, p, *unperms)


def _merge_kernel(ya_ref, yb_ref, wa_ref, wb_ref, ga_ref, gb_ref, o_ref, *, heads_per_tile):
    acc_a = _dot(ya_ref[0], wa_ref[...])
    acc_b = _dot(yb_ref[0], wb_ref[...])
    for k in range(heads_per_tile):
        sl = slice(k * LANE, (k + 1) * LANE)
        ga = jax.nn.sigmoid(ga_ref[0, k].astype(F32))
        gb = jax.nn.sigmoid(gb_ref[0, k].astype(F32))
        o_ref[0, :, sl] = (ga * acc_a[:, sl] + gb * acc_b[:, sl]).astype(BF16)


def _merge(ya, yb, wa, wb, p, hga, hgb):
    b, s, ka = ya.shape
    kb = yb.shape[2]
    d = wa.shape[1]
    tm = _largest_tile(s, (1024, 512, 256, 128))
    tn = _largest_tile(math.gcd(d, hga * LANE, hgb * LANE), (1024, 512, 256, 128))
    hpt = tn // LANE
    return pl.pallas_call(
        functools.partial(_merge_kernel, heads_per_tile=hpt),
        grid=(b, s // tm, d // tn),
        in_specs=[
            pl.BlockSpec((1, tm, ka), lambda bi, i, j: (bi, i, 0)),
            pl.BlockSpec((1, tm, kb), lambda bi, i, j: (bi, i, 0)),
            pl.BlockSpec((ka, tn), lambda bi, i, j: (0, j)),
            pl.BlockSpec((kb, tn), lambda bi, i, j: (0, j)),
            pl.BlockSpec((1, hpt, tm, LANE), lambda bi, i, j: (bi, hga // hpt + j, i, 0)),
            pl.BlockSpec((1, hpt, tm, LANE), lambda bi, i, j: (bi, hgb // hpt + j, i, 0)),
        ],
        out_specs=pl.BlockSpec((1, tm, tn), lambda bi, i, j: (bi, i, j)),
        out_shape=jax.ShapeDtypeStruct((b, s, d), BF16),
        compiler_params=_params("parallel", "parallel", "arbitrary"),
        name="branch_merge",
    )(ya, yb, wa, wb, p, p)


def _final_kernel(*refs, with_next):
    if with_next:
        x_ref, out_ref, gate_ref, gpost_ref, sh_ref, sc_ref, gpre_ref, y_ref, h_ref = refs
    else:
        x_ref, out_ref, gate_ref, gpost_ref, y_ref = refs
    out = out_ref[0]
    normed = out * lax.rsqrt(jnp.mean(out * out, axis=-1, keepdims=True) + EPS) * gpost_ref[...]
    y = x_ref[0] + gate_ref[0] * normed
    y_ref[0] = y
    if with_next:
        h_ref[0] = _modulated_norm(y, gpre_ref[...], sc_ref[0], sh_ref[0]).astype(BF16)


def _finalize(x, out, mod, g_post, next_mod, next_gain):
    b, s, d = x.shape
    ts = _largest_tile(s, (256, 128, 64, 32, 16, 8))
    with_next = next_mod is not None
    tile = pl.BlockSpec((1, ts, d), lambda bi, i: (bi, i, 0))
    row = pl.BlockSpec((1, d), lambda bi, i: (0, 0))

    def mod_spec(part):
        return pl.BlockSpec((1, 1, d), lambda bi, i, part=part: (bi, 0, part))

    in_specs = [tile, tile, mod_spec(2), row]
    args = [x, out, mod, g_post.reshape(1, d)]
    out_specs = [tile]
    out_shape = [jax.ShapeDtypeStruct((b, s, d), F32)]
    if with_next:
        in_specs += [mod_spec(0), mod_spec(1), row]
        args += [next_mod, next_mod, next_gain.reshape(1, d)]
        out_specs.append(tile)
        out_shape.append(jax.ShapeDtypeStruct((b, s, d), BF16))
    res = pl.pallas_call(
        functools.partial(_final_kernel, with_next=with_next),
        grid=(b, s // ts),
        in_specs=in_specs,
        out_specs=out_specs,
        out_shape=out_shape,
        compiler_params=_params("parallel", "parallel"),
        name="postnorm_residual",
    )(*args)
    return (res[0], res[1]) if with_next else (res[0], None)


def _rope_tables(s):
    half = LANE // 2
    inv = ROPE_THETA ** (-jnp.arange(half, dtype=F32) / half)
    ang = jnp.arange(s, dtype=F32)[:, None] * inv[None, :]
    cos, sin = jnp.cos(ang), jnp.sin(ang)
    return jnp.concatenate([cos, cos], axis=1), jnp.concatenate([-sin, sin], axis=1)


def _pad_row(v):
    v = v.reshape(1, -1).astype(F32)
    return jnp.pad(v, ((0, 0), (0, LANE - v.shape[1])))


def kernel(x_prompt, x_sample, c_prompt, c_sample, w_ada, b_ada, norm_pre, norm_post, w_in, conv_a,
           a_log, dt_bias, norm_a, w_up_a, w_up_b, w_out):
    depth, d, _ = w_in.shape
    ah = d // 256
    bh = d // 512
    aw = ah * LANE
    n_ab = 4 * ah
    assert n_ab <= LANE
    hq_b = 4 * ah
    hk_b = hq_b + 3 * bh
    hv_b = hk_b + 3 * bh
    hz_b = hv_b + 3 * bh
    hg_a = hz_b + bh
    hg_b = hg_a + d // LANE
    c0 = 4 * aw
    c1 = c0 + n_ab

    bp, bs = c_prompt.shape[0], c_sample.shape[0]
    c_all = jnp.concatenate([c_prompt, c_sample], axis=0)
    c_all = jnp.pad(c_all, ((0, -c_all.shape[0] % 8), (0, 0)))
    mod_all = _modulation(c_all, w_ada, b_ada)

    w_main = [jnp.concatenate([w_in[l, :, :c0], w_in[l, :, c1:]], axis=1).astype(BF16) for l in range(depth)]
    w_ab = [jnp.pad(w_in[l, :, c0:c1], ((0, 0), (0, LANE - n_ab))).astype(BF16) for l in range(depth)]
    wa = [w_up_a[l].astype(BF16) for l in range(depth)]
    wb = [w_up_b[l].astype(BF16) for l in range(depth)]
    wo = [w_out[l].astype(BF16) for l in range(depth)]
    perms = [_residue_permutation(dil) for _, dil in B_GROUPS if dil > 1]
    unperms = [pm.T for pm in perms]

    def trunk(x, row0, nb):
        s = x.shape[1]
        cos_t, sin_t = _rope_tables(s)
        mods = [mod_all[l, row0:row0 + nb].reshape(nb, 1, 3 * d) for l in range(depth)]
        h = _prenorm(x, mods[0], norm_pre[0])
        for l in range(depth):
            p = _inproj(h, w_main[l], cos_t, sin_t, perms, hq_b, hv_b, hz_b, bh)
            ab = _matmul(h, w_ab[l], F32, "in_proj_gates")
            gates = _gates(ab, _pad_row(a_log[l]), _pad_row(dt_bias[l]), ah)
            ya = _gdn(p, gates, conv_a[l], norm_a[l].reshape(1, LANE), ah)
            outs, lses = [], []
            for gi, (window, dilation) in enumerate(B_GROUPS):
                o, lse = _window_attention(p, hq_b + gi * bh, hk_b + gi * bh, hv_b + gi * bh, bh,
                                           window, dilation)
                outs.append(o)
                lses.append(lse)
            yb = _combine(outs, lses, unperms, p, hz_b, bh)
            merged = _merge(ya, yb, wa[l], wb[l], p, hg_a, hg_b)
            out = _matmul(merged, wo[l], F32, "out_proj")
            last = l == depth - 1
            x, h = _finalize(x, out, mods[l], norm_post[l],
                             None if last else mods[l + 1], None if last else norm_pre[l + 1])
        return x

    y_prompt = trunk(x_prompt, 0, bp)
    y_sample = trunk(x_sample, bp, bs)
    return (y_prompt, y_sample)
```

```python
import functools
import math

import jax
import jax.numpy as jnp
from jax import lax
from jax.experimental import pallas as pl
from jax.experimental.pallas import tpu as pltpu

F32 = jnp.float32
BF16 = jnp.bfloat16

LANE = 128
BF16_SUBLANES = 16
V7X_VMEM_BYTES = 64 * 1024 * 1024
VMEM_LIMIT_BYTES = V7X_VMEM_BYTES - 8 * 1024 * 1024

EPS = 1e-6
L2_EPS = 1e-6
ROPE_THETA = 10000.0
NEG_INF = -1e30
CONV_K = 5
B_GROUPS = ((128, 1), (512, 4), (2048, 16))
GDN_CHUNK = 128
GDN_CHUNKS_PER_STEP = 4
ATTN_TQ = 128
ATTN_BLOCKS_PER_STEP = 4
PERM_ROWS = 256


def _params(*sem):
    return pltpu.CompilerParams(dimension_semantics=sem, vmem_limit_bytes=VMEM_LIMIT_BYTES)


def _largest_tile(n, candidates):
    for c in candidates:
        if n % c == 0:
            return c
    raise ValueError(f"no tile in {candidates} divides {n}")


def _dot(a, b):
    return jnp.dot(a, b, preferred_element_type=F32)


def _dot_exact(a, b):
    return jnp.dot(a, b, precision=lax.Precision.HIGHEST, preferred_element_type=F32)


def _dot_nt(a, b):
    return lax.dot_general(a, b, (((1,), (1,)), ((), ())), preferred_element_type=F32)


def _silu(x):
    return x * jax.nn.sigmoid(x)


def _single_buffered(shape, index_map):
    return pl.BlockSpec(shape, index_map, pipeline_mode=pl.Buffered(1))


def _mod_kernel(c_ref, w_ref, b_ref, o_ref):
    c = c_ref[...]
    o_ref[0] = _dot(_silu(c).astype(BF16), w_ref[0].astype(BF16)) + b_ref[0]


def _modulation(c_all, w_ada, b_ada):
    depth, d, n = w_ada.shape
    r = c_all.shape[0]
    tn = _largest_tile(n, (512, 256, 128))
    return pl.pallas_call(
        _mod_kernel,
        grid=(depth, n // tn),
        in_specs=[
            pl.BlockSpec((r, d), lambda l, j: (0, 0)),
            pl.BlockSpec((1, d, tn), lambda l, j: (l, 0, j)),
            pl.BlockSpec((1, 1, tn), lambda l, j: (l, 0, j)),
        ],
        out_specs=pl.BlockSpec((1, r, tn), lambda l, j: (l, 0, j)),
        out_shape=jax.ShapeDtypeStruct((depth, r, n), F32),
        compiler_params=_params("parallel", "parallel"),
        name="adaln_mod",
    )(c_all, w_ada, b_ada.reshape(depth, 1, n))


def _modulated_norm(x, gain, scale, shift):
    y = x * lax.rsqrt(jnp.mean(x * x, axis=-1, keepdims=True) + EPS)
    return (y * gain) * (1.0 + scale) + shift


def _prenorm_kernel(x_ref, sh_ref, sc_ref, g_ref, o_ref):
    o_ref[0] = _modulated_norm(x_ref[0], g_ref[...], sc_ref[0], sh_ref[0]).astype(BF16)


def _prenorm(x, mod, gain):
    b, s, d = x.shape
    ts = _largest_tile(s, (256, 128, 64, 32, 16, 8))
    return pl.pallas_call(
        _prenorm_kernel,
        grid=(b, s // ts),
        in_specs=[
            pl.BlockSpec((1, ts, d), lambda bi, i: (bi, i, 0)),
            pl.BlockSpec((1, 1, d), lambda bi, i: (bi, 0, 0)),
            pl.BlockSpec((1, 1, d), lambda bi, i: (bi, 0, 1)),
            pl.BlockSpec((1, d), lambda bi, i: (0, 0)),
        ],
        out_specs=pl.BlockSpec((1, ts, d), lambda bi, i: (bi, i, 0)),
        out_shape=jax.ShapeDtypeStruct((b, s, d), BF16),
        compiler_params=_params("parallel", "parallel"),
        name="prenorm",
    )(x, mod, mod, gain.reshape(1, d))


def _residue_permutation(dilation):
    i = jnp.arange(PERM_ROWS)
    src = (i % (PERM_ROWS // dilation)) * dilation + i // (PERM_ROWS // dilation)
    return (src[:, None] == jnp.arange(PERM_ROWS)[None, :]).astype(BF16)


def _inproj_kernel(h_ref, w_ref, cos_ref, sin_ref, *rest, heads_per_tile, hq_b, hv_b, hz_b, bh, dilations):
    perm_refs, o_ref = rest[:-1], rest[-1]
    tm = h_ref.shape[1]
    acc = _dot(h_ref[0], w_ref[...])
    perm_of = {}
    for d in dilations:
        if d > 1:
            perm_of[d] = perm_refs[len(perm_of)]
    head0 = pl.program_id(2) * heads_per_tile
    in_b = jnp.logical_and(head0 >= hq_b, head0 < hz_b)
    group = jnp.where(in_b, ((head0 - hq_b) // bh) % len(dilations), 0)
    is_rope = jnp.logical_and(head0 >= hq_b, head0 < hv_b)
    for k in range(heads_per_tile):
        o_ref[0, k] = acc[:, k * LANE:(k + 1) * LANE].astype(BF16)
    for rope in (False, True):
        rope_matches = is_rope if rope else jnp.logical_not(is_rope)
        for g, d in enumerate(dilations):
            if not rope and d == 1:
                continue

            @pl.when(jnp.logical_and(rope_matches, group == g))
            def _(rope=rope, d=d):
                slabs = []
                for k in range(heads_per_tile):
                    slab = acc[:, k * LANE:(k + 1) * LANE]
                    if rope:
                        slab = slab * cos_ref[...] + pltpu.roll(slab, LANE // 2, 1) * sin_ref[...]
                    slabs.append(slab.astype(BF16))
                if d == 1:
                    for k in range(heads_per_tile):
                        o_ref[0, k] = slabs[k]
                else:
                    tile = jnp.concatenate(slabs, axis=1)
                    for s in range(tm // PERM_ROWS):
                        rows = slice(s * PERM_ROWS, (s + 1) * PERM_ROWS)
                        moved = _dot(perm_of[d][...], tile[rows, :]).astype(BF16)
                        for k in range(heads_per_tile):
                            o_ref[0, k, rows, :] = moved[:, k * LANE:(k + 1) * LANE]


def _inproj(h, w, cos_t, sin_t, perms, hq_b, hv_b, hz_b, bh):
    b, s, d = h.shape
    n = w.shape[1]
    tm = _largest_tile(s, (1024, 512, 256))
    tn = _largest_tile(math.gcd(n, bh * LANE), (1024, 512, 256, 128))
    hpt = tn // LANE
    assert hq_b % hpt == 0
    dilations = tuple(dil for _, dil in B_GROUPS)
    kern = functools.partial(_inproj_kernel, heads_per_tile=hpt, hq_b=hq_b, hv_b=hv_b, hz_b=hz_b, bh=bh,
                             dilations=dilations)
    perm_spec = pl.BlockSpec((PERM_ROWS, PERM_ROWS), lambda bi, i, j: (0, 0))
    return pl.pallas_call(
        kern,
        grid=(b, s // tm, n // tn),
        in_specs=[
            pl.BlockSpec((1, tm, d), lambda bi, i, j: (bi, i, 0)),
            pl.BlockSpec((d, tn), lambda bi, i, j: (0, j)),
            pl.BlockSpec((tm, LANE), lambda bi, i, j: (i, 0)),
            pl.BlockSpec((tm, LANE), lambda bi, i, j: (i, 0)),
        ] + [perm_spec] * len(perms),
        out_specs=pl.BlockSpec((1, hpt, tm, LANE), lambda bi, i, j: (bi, j, i, 0)),
        out_shape=jax.ShapeDtypeStruct((b, n // LANE, s, LANE), BF16),
        compiler_params=_params("parallel", "parallel", "arbitrary"),
        name="in_proj",
    )(h, w, cos_t, sin_t, *perms)


def _mm_kernel(x_ref, w_ref, o_ref):
    o_ref[0] = _dot(x_ref[0], w_ref[...]).astype(o_ref.dtype)


def _matmul(x, w, out_dtype, name):
    b, s, k = x.shape
    n = w.shape[1]
    tm = _largest_tile(s, (1024, 512, 256, 128))
    tn = _largest_tile(n, (1024, 512, 256, 128))
    return pl.pallas_call(
        _mm_kernel,
        grid=(b, s // tm, n // tn),
        in_specs=[
            pl.BlockSpec((1, tm, k), lambda bi, i, j: (bi, i, 0)),
            pl.BlockSpec((k, tn), lambda bi, i, j: (0, j)),
        ],
        out_specs=pl.BlockSpec((1, tm, tn), lambda bi, i, j: (bi, i, j)),
        out_shape=jax.ShapeDtypeStruct((b, s, n), out_dtype),
        compiler_params=_params("parallel", "parallel", "arbitrary"),
        name=name,
    )(x, w)


def _gate_kernel(ab_ref, alog_ref, dtb_ref, o_ref, *, chunk, heads, tiles):
    C, H = chunk, heads
    row = lax.broadcasted_iota(jnp.int32, (C, C), 0)
    col = lax.broadcasted_iota(jnp.int32, (C, C), 1)
    lane = lax.broadcasted_iota(jnp.int32, (1, LANE), 1)
    lower = (row >= col).astype(F32)
    upper = (row <= col).astype(F32)
    for t in range(tiles):
        rows = slice(t * C, (t + 1) * C)
        ab = ab_ref[0, rows, :]
        gval = -jnp.exp(alog_ref[...]) * jax.nn.softplus(ab + dtb_ref[...])
        prefix = _dot_exact(lower, gval)
        suffix = _dot_exact(upper, gval)
        o_ref[0, rows, :] = jnp.where(lane < H, prefix, jnp.where(lane < 2 * H, suffix, jax.nn.sigmoid(ab)))


def _gates(ab, alog_row, dtb_row, heads):
    b, s, _ = ab.shape
    c = GDN_CHUNK
    tiles = 2 if s % (2 * c) == 0 else 1
    ts = tiles * c
    row_spec = pl.BlockSpec((1, LANE), lambda bi, i: (0, 0))
    tile = pl.BlockSpec((1, ts, LANE), lambda bi, i: (bi, i, 0))
    return pl.pallas_call(
        functools.partial(_gate_kernel, chunk=c, heads=heads, tiles=tiles),
        grid=(b, s // ts),
        in_specs=[tile, row_spec, row_spec],
        out_specs=tile,
        out_shape=jax.ShapeDtypeStruct((b, s, LANE), F32),
        compiler_params=_params("parallel", "parallel"),
        name="deltanet_gates",
    )(ab, alog_row, dtb_row)


def _tri_inverse_stages(lmats, n, result):
    row = lax.broadcasted_iota(jnp.int32, (n, n), 0)
    col = lax.broadcasted_iota(jnp.int32, (n, n), 1)
    eye = jnp.where(row == col, 1.0, 0.0)
    pair = (row >> 1) == (col >> 1)
    ts = [eye - jnp.where(pair, lm, 0.0) for lm in lmats]
    for lv in range(1, int(math.log2(n))):
        join = jnp.logical_and((row >> (lv + 1)) == (col >> (lv + 1)), (row >> lv) != (col >> lv))
        tbs = [t.astype(BF16) for t in ts]
        mids = [_dot(tb, jnp.where(join, lm, 0.0).astype(BF16)) for tb, lm in zip(tbs, lmats)]
        yield
        ts = [t - _dot(mid.astype(BF16), tb) for t, mid, tb in zip(ts, mids, tbs)]
        yield
    result.extend(ts)


def _emit_interleaved(generators):
    active = list(generators)
    while active:
        for gen in list(active):
            try:
                next(gen)
            except StopIteration:
                active.remove(gen)


def _gdn_kernel(q_ref, k_ref, v_ref, z_ref, gate_ref, cq_ref, ck_ref, cv_ref, nw_ref, o_ref,
                oacc, cstage, qn, kn, vn, us, ws, qds, kdt, att, gls, *, seq, chunk, heads):
    S, C, H = seq, chunk, heads
    nc = S // C
    G = GDN_CHUNKS_PER_STEP
    R = C
    nr = S // R
    halo = BF16_SUBLANES
    h = pl.program_id(1)
    lane = lax.broadcasted_iota(jnp.int32, (1, LANE), 1)
    row = lax.broadcasted_iota(jnp.int32, (C, C), 0)
    col = lax.broadcasted_iota(jnp.int32, (C, C), 1)

    def conv_tile(slot, src_ref, cw_ref, dst_ref, norm_scale, r):
        r0 = pl.multiple_of(r * R, R)
        prev0 = pl.multiple_of(jnp.maximum(r0 - halo, 0), halo)
        next0 = pl.multiple_of(jnp.minimum(r0 + R, S - halo), halo)
        prev = src_ref[0, 0, pl.ds(prev0, halo), :].astype(F32)
        nxt = src_ref[0, 0, pl.ds(next0, halo), :].astype(F32)
        cstage[slot, 0:halo, :] = jnp.where(r > 0, prev, 0.0)
        cstage[slot, halo:halo + R, :] = src_ref[0, 0, pl.ds(r0, R), :].astype(F32)
        cstage[slot, halo + R:, :] = jnp.where(r < nr - 1, nxt, 0.0)
        y = jnp.zeros((R, LANE), F32)
        for j in range(CONV_K):
            off = halo - CONV_K // 2 + j
            y = y + cw_ref[j:j + 1, :] * cstage[slot, off:off + R, :]
        y = _silu(y)
        if norm_scale is not None:
            y = y * (lax.rsqrt(jnp.sum(y * y, axis=-1, keepdims=True) + L2_EPS) * norm_scale)
        dst_ref[pl.ds(r0, R), :] = y.astype(dst_ref.dtype)

    def conv_body(r, carry):
        conv_tile(0, q_ref, cq_ref, qn, LANE ** -0.5, r)
        conv_tile(1, k_ref, ck_ref, kn, 1.0, r)
        conv_tile(2, v_ref, cv_ref, vn, None, r)
        oacc[pl.ds(pl.multiple_of(r * R, R), R), :] = jnp.zeros((R, LANE), F32)
        return carry

    lax.fori_loop(0, nr, conv_body, 0)

    ng = nc // G
    masks = [((row >= col), (row > col)), ((row <= col), (row < col))]

    def chain_chunks(i):
        return [(0, g, i * G + g) for g in range(G)] + [(1, g, nc - (i + 1) * G + g) for g in range(G)]

    def prepare_stages(i, slot):
        items = []
        for dirn, g, c in chain_chunks(i):
            c0 = pl.multiple_of(c * C, C)
            items.append((dirn, g, qn[pl.ds(c0, C), :], kn[pl.ds(c0, C), :],
                          vn[pl.ds(c0, C), :].astype(F32), gate_ref[0, pl.ds(c0, C), :]))
        kks = [_dot_nt(kb, kb) for (_, _, _, kb, _, _) in items]
        qks = [_dot_nt(qb, kb) for (_, _, qb, kb, _, _) in items]
        yield
        chains = []
        for (dirn, g, qb, kb, vc, gt), kk, qk in zip(items, kks, qks):
            incl, strict = masks[dirn]
            gc = jnp.sum(jnp.where(lane == dirn * H + h, gt, 0.0), axis=1, keepdims=True)
            beta = jnp.sum(jnp.where(lane == (2 + dirn) * H + h, gt, 0.0), axis=1, keepdims=True)
            gcb = jnp.broadcast_to(gc, (C, C))
            decay = jnp.exp(jnp.where(incl, gcb - gcb.T, NEG_INF))
            lmat = jnp.where(strict, (beta * kk) * decay, 0.0)
            amat = jnp.where(incl, qk * decay, 0.0)
            chains.append((dirn, g, qb, kb, vc, gc, beta, amat, lmat))
        tinvs = []
        yield from _tri_inverse_stages([ch[8] for ch in chains], C, tinvs)
        rhss = []
        for (dirn, g, qb, kb, vc, gc, beta, amat, lmat) in chains:
            rhs = jnp.concatenate([vc * beta, kb.astype(F32) * (beta * jnp.exp(gc))], axis=1)
            rhss.append(rhs.astype(BF16))
        uws = [_dot(tinv.astype(BF16), rhs) for tinv, rhs in zip(tinvs, rhss)]
        yield
        for (dirn, g, qb, kb, vc, gc, beta, amat, lmat), uw in zip(chains, uws):
            g_last = gc[C - 1:C, :] if dirn == 0 else gc[0:1, :]
            kd = kb.astype(F32) * jnp.exp(g_last - gc)
            rows = slice(g * C, (g + 1) * C)
            us[slot, dirn, rows, :] = uw[:, :LANE].astype(BF16)
            ws[slot, dirn, rows, :] = uw[:, LANE:].astype(BF16)
            qds[slot, dirn, rows, :] = (qb.astype(F32) * jnp.exp(gc)).astype(BF16)
            kdt[slot, dirn, g * LANE:(g + 1) * LANE, :] = kd.T.astype(BF16)
            att[slot, dirn, rows, :] = amat.astype(BF16)
            gls[slot, dirn, g:g + 1, :] = jnp.broadcast_to(jnp.exp(g_last), (1, LANE))

    def update_stages(i, slot, states, result):
        states = list(states)
        for step in range(G):
            gs = (step, G - 1 - step)
            cs = (i * G + gs[0], nc - (i + 1) * G + gs[1])
            rows = [slice(g * C, (g + 1) * C) for g in gs]
            sbs = [st.astype(BF16) for st in states]
            wss = [_dot(ws[slot, d, rows[d], :], sbs[d]) for d in (0, 1)]
            qss = [_dot(qds[slot, d, rows[d], :], sbs[d]) for d in (0, 1)]
            yield
            vbs = [(us[slot, d, rows[d], :].astype(F32) - wss[d]).astype(BF16) for d in (0, 1)]
            avs = [_dot(att[slot, d, rows[d], :], vbs[d]) for d in (0, 1)]
            kvs = [_dot(kdt[slot, d, gs[d] * LANE:(gs[d] + 1) * LANE, :], vbs[d]) for d in (0, 1)]
            yield
            for d in (0, 1):
                out_rows = pl.ds(pl.multiple_of(cs[d] * C, C), C)
                oacc[out_rows, :] = oacc[out_rows, :] + (qss[d] + avs[d])
            states = [states[d] * gls[slot, d, gs[d]:gs[d] + 1, :] + kvs[d] for d in (0, 1)]
        result.extend(states)

    _emit_interleaved([prepare_stages(0, 0)])

    def pipelined_body(i, states):
        new_states = []
        _emit_interleaved([prepare_stages(i, i % 2), update_stages(i - 1, (i - 1) % 2, states, new_states)])
        return tuple(new_states)

    zero_state = jnp.zeros((LANE, LANE), F32)
    states = lax.fori_loop(1, ng, pipelined_body, (zero_state, zero_state))
    _emit_interleaved([update_stages(ng - 1, (ng - 1) % 2, states, [])])

    def finish_tile(r):
        r0 = pl.multiple_of(r * R, R)
        o = oacc[pl.ds(r0, R), :]
        y = o * lax.rsqrt(jnp.mean(o * o, axis=-1, keepdims=True) + EPS) * nw_ref[...]
        z = z_ref[0, 0, pl.ds(r0, R), :].astype(F32)
        o_ref[0, pl.ds(r0, R), :] = (y * _silu(z)).astype(BF16)

    def finish_body(i, carry):
        finish_tile(2 * i)
        finish_tile(2 * i + 1)
        return carry

    lax.fori_loop(0, nr // 2, finish_body, 0)


def _gdn(p, gates, conv_w, norm_w, heads):
    b, _, s, _ = p.shape
    c = GDN_CHUNK
    nc = s // c
    g = GDN_CHUNKS_PER_STEP
    assert s % (2 * c) == 0 and nc % (2 * g) == 0
    hq, hk, hv, hz = 0, heads, 2 * heads, 3 * heads

    def head_spec(off):
        return pl.BlockSpec((1, 1, s, LANE), lambda bi, hi, off=off: (bi, off + hi, 0, 0))

    def conv_spec(off):
        return pl.BlockSpec((CONV_K, LANE), lambda bi, hi, off=off: (0, off + hi))

    kern = functools.partial(_gdn_kernel, seq=s, chunk=c, heads=heads)
    return pl.pallas_call(
        kern,
        grid=(b, heads),
        in_specs=[
            head_spec(hq), head_spec(hk), head_spec(hv), head_spec(hz),
            _single_buffered((1, s, LANE), lambda bi, hi: (bi, 0, 0)),
            conv_spec(hq), conv_spec(hk), conv_spec(hv),
            pl.BlockSpec((1, LANE), lambda bi, hi: (0, 0)),
        ],
        out_specs=pl.BlockSpec((1, s, LANE), lambda bi, hi: (bi, 0, hi)),
        out_shape=jax.ShapeDtypeStruct((b, s, heads * LANE), BF16),
        scratch_shapes=[
            pltpu.VMEM((s, LANE), F32),
            pltpu.VMEM((3, c + 2 * BF16_SUBLANES, LANE), F32),
            pltpu.VMEM((s, LANE), BF16),
            pltpu.VMEM((s, LANE), BF16),
            pltpu.VMEM((s, LANE), BF16),
            pltpu.VMEM((2, 2, g * c, LANE), BF16),
            pltpu.VMEM((2, 2, g * c, LANE), BF16),
            pltpu.VMEM((2, 2, g * c, LANE), BF16),
            pltpu.VMEM((2, 2, g * LANE, c), BF16),
            pltpu.VMEM((2, 2, g * c, c), BF16),
            pltpu.VMEM((2, 2, max(g, 8), LANE), F32),
        ],
        compiler_params=_params("parallel", "arbitrary"),
        name="gated_deltanet",
    )(p, p, p, p, gates, conv_w, conv_w, conv_w, norm_w)


def _attend_many(blocks, reach):
    scores = [_dot_nt(q, k) * (LANE ** -0.5) for (q, k, _, _, _) in blocks]
    probs, stats = [], []
    for s, (q, k, _, q0, k0) in zip(scores, blocks):
        tq, w = q.shape[0], k.shape[0]
        qpos = q0 + lax.broadcasted_iota(jnp.int32, (tq, w), 0)
        kpos = k0 + lax.broadcasted_iota(jnp.int32, (tq, w), 1)
        s = jnp.where(jnp.abs(kpos - qpos) <= reach, s, NEG_INF)
        m = jnp.max(s, axis=1, keepdims=True)
        p = jnp.exp(s - m)
        den = jnp.sum(p, axis=1, keepdims=True)
        probs.append(p.astype(BF16))
        stats.append((m, den))
    outs = [_dot(p, v) for p, (_, _, v, _, _) in zip(probs, blocks)]
    return [((o / den).astype(BF16), jnp.broadcast_to(m + jnp.log(den), (o.shape[0], LANE)))
            for o, (m, den) in zip(outs, stats)]


def _attn_kernel(q_ref, k_ref, v_ref, o_ref, lse_ref, *scratch, length, reach, rows, dilation, unroll):
    L, R, TQ, U = length, reach, ATTN_TQ, unroll
    W = TQ + 2 * R

    def residue(r, carry):
        if rows:
            qs, ks, vs = scratch

            def gather(g, c):
                dst = pl.ds(pl.multiple_of(g * rows, rows), rows)
                qs[dst, :] = q_ref[0, 0, g, r]
                ks[dst, :] = k_ref[0, 0, g, r]
                vs[dst, :] = v_ref[0, 0, g, r]
                return c

            lax.fori_loop(0, L // rows, gather, 0)

        def body(step, c):
            blocks = []
            for u in range(U):
                i = step * U + u
                q0 = pl.multiple_of(i * TQ, TQ)
                k0 = pl.multiple_of(jnp.clip(q0 - R, 0, L - W), R)
                if rows:
                    blocks.append((qs[pl.ds(q0, TQ), :], ks[pl.ds(k0, W), :], vs[pl.ds(k0, W), :], q0, k0))
                else:
                    blocks.append((q_ref[0, 0, pl.ds(q0, TQ), :], k_ref[0, 0, pl.ds(k0, W), :],
                                   v_ref[0, 0, pl.ds(k0, W), :], q0, k0))
            for u, (o, lse) in enumerate(_attend_many(blocks, R)):
                i = step * U + u
                if rows:
                    for j in range(TQ // rows):
                        part = slice(j * rows, (j + 1) * rows)
                        o_ref[0, 0, i * (TQ // rows) + j, r] = o[part, :]
                        lse_ref[0, 0, i * (TQ // rows) + j, r] = lse[part, :]
                else:
                    q0 = pl.multiple_of(i * TQ, TQ)
                    o_ref[0, 0, pl.ds(q0, TQ), :] = o
                    lse_ref[0, 0, pl.ds(q0, TQ), :] = lse
            return c

        lax.fori_loop(0, L // TQ // U, body, 0)
        return carry

    if rows:
        lax.fori_loop(0, dilation, residue, 0)
    else:
        residue(0, 0)


def _window_attention(p, hq, hk, hv, heads, window, dilation):
    b, nh, s, _ = p.shape
    reach = window // (2 * dilation)
    length = s // dilation
    unroll = math.gcd(ATTN_BLOCKS_PER_STEP, length // ATTN_TQ)
    assert length % ATTN_TQ == 0 and length >= ATTN_TQ + 2 * reach
    if dilation == 1:
        rows = 0
        pv = p
        out_dims = (b, heads, s, LANE)
        block = (1, 1, s, LANE)
        tail = (0, 0)
        scratch = []
    else:
        rows = PERM_ROWS // dilation
        assert rows % BF16_SUBLANES == 0 and ATTN_TQ % rows == 0 and s % PERM_ROWS == 0
        ng = s // PERM_ROWS
        pv = p.reshape(b, nh, ng, dilation, rows, LANE)
        out_dims = (b, heads, ng, dilation, rows, LANE)
        block = (1, 1, ng, dilation, rows, LANE)
        tail = (0, 0, 0, 0)
        scratch = [pltpu.VMEM((length, LANE), BF16)] * 3

    def spec(off):
        return pl.BlockSpec(block, lambda bi, hi, off=off: (bi, off + hi) + tail)

    kern = functools.partial(_attn_kernel, length=length, reach=reach, rows=rows, dilation=dilation,
                             unroll=unroll)
    o, lse = pl.pallas_call(
        kern,
        grid=(b, heads),
        in_specs=[spec(hq), spec(hk), spec(hv)],
        out_specs=[spec(0), spec(0)],
        out_shape=[jax.ShapeDtypeStruct(out_dims, BF16), jax.ShapeDtypeStruct(out_dims, F32)],
        scratch_shapes=scratch,
        compiler_params=_params("parallel", "parallel"),
        name=f"window_attention_d{dilation}",
    )(pv, pv, pv)
    return o.reshape(b, heads, s, LANE), lse.reshape(b, heads, s, LANE)


def _select_rows_f32(sel, x):
    hi = x.astype(BF16)
    rest = x - hi.astype(F32)
    mid = rest.astype(BF16)
    lo = (rest - mid.astype(F32)).astype(BF16)
    return _dot(sel, hi) + _dot(sel, mid) + _dot(sel, lo)


def _combine_kernel(*refs, heads, dilations):
    ng = len(dilations)
    o_refs, l_refs = refs[:ng], refs[ng:2 * ng]
    z_ref = refs[2 * ng]
    unperm_refs = refs[2 * ng + 1:-1]
    y_ref = refs[-1]
    unperm_of = {}
    for d in dilations:
        if d > 1:
            unperm_of[d] = unperm_refs[len(unperm_of)]
    for h in range(heads):
        outs, lses = [], []
        for g, d in enumerate(dilations):
            o, lse = o_refs[g][0, h], l_refs[g][0, h]
            if d > 1:
                o = _dot(unperm_of[d][...], o)
                lse = _select_rows_f32(unperm_of[d][...], lse)
            else:
                o = o.astype(F32)
            outs.append(o)
            lses.append(lse)
        m = functools.reduce(jnp.maximum, lses)
        wts = [jnp.exp(l - m) for l in lses]
        den = functools.reduce(lambda a, c: a + c, wts)
        acc = functools.reduce(lambda a, c: a + c, [w * o for w, o in zip(wts, outs)])
        z = z_ref[0, h].astype(F32)
        y_ref[0, :, h * LANE:(h + 1) * LANE] = (acc / den * _silu(z)).astype(BF16)


def _combine(outs, lses, unperms, p, hz, heads):
    b, _, s, _ = p.shape
    ts = PERM_ROWS
    assert hz % heads == 0 and s % ts == 0
    dilations = tuple(dil for _, dil in B_GROUPS)
    spec = pl.BlockSpec((1, heads, ts, LANE), lambda bi, i: (bi, 0, i, 0))
    n = len(outs)
    return pl.pallas_call(
        functools.partial(_combine_kernel, heads=heads, dilations=dilations),
        grid=(b, s // ts),
        in_specs=[spec] * (2 * n)
        + [pl.BlockSpec((1, heads, ts, LANE), lambda bi, i: (bi, hz // heads, i, 0))]
        + [pl.BlockSpec((PERM_ROWS, PERM_ROWS), lambda bi, i: (0, 0))] * len(unperms),
        out_specs=pl.BlockSpec((1, ts, heads * LANE), lambda bi, i: (bi, i, 0)),
        out_shape=jax.ShapeDtypeStruct((b, s, heads * LANE), BF16),
        compiler_params=_params("parallel", "parallel"),
        name="group_combine",
    )(*outs, *lses, p, *unperms)


def _merge_kernel(ya_ref, yb_ref, wa_ref, wb_ref, ga_ref, gb_ref, o_ref, *, heads_per_tile):
    acc_a = _dot(ya_ref[0], wa_ref[...])
    acc_b = _dot(yb_ref[0], wb_ref[...])
    for k in range(heads_per_tile):
        sl = slice(k * LANE, (k + 1) * LANE)
        ga = jax.nn.sigmoid(ga_ref[0, k].astype(F32))
        gb = jax.nn.sigmoid(gb_ref[0, k].astype(F32))
        o_ref[0, :, sl] = (ga * acc_a[:, sl] + gb * acc_b[:, sl]).astype(BF16)


def _merge(ya, yb, wa, wb, p, hga, hgb):
    b, s, ka = ya.shape
    kb = yb.shape[2]
    d = wa.shape[1]
    tm = _largest_tile(s, (1024, 512, 256, 128))
    tn = _largest_tile(math.gcd(d, hga * LANE, hgb * LANE), (1024, 512, 256, 128))
    hpt = tn // LANE
    return pl.pallas_call(
        functools.partial(_merge_kernel, heads_per_tile=hpt),
        grid=(b, s // tm, d // tn),
        in_specs=[
            pl.BlockSpec((1, tm, ka), lambda bi, i, j: (bi, i, 0)),
            pl.BlockSpec((1, tm, kb), lambda bi, i, j: (bi, i, 0)),
            pl.BlockSpec((ka, tn), lambda bi, i, j: (0, j)),
            pl.BlockSpec((kb, tn), lambda bi, i, j: (0, j)),
            pl.BlockSpec((1, hpt, tm, LANE), lambda bi, i, j: (bi, hga // hpt + j, i, 0)),
            pl.BlockSpec((1, hpt, tm, LANE), lambda bi, i, j: (bi, hgb // hpt + j, i, 0)),
        ],
        out_specs=pl.BlockSpec((1, tm, tn), lambda bi, i, j: (bi, i, j)),
        out_shape=jax.ShapeDtypeStruct((b, s, d), BF16),
        compiler_params=_params("parallel", "parallel", "arbitrary"),
        name="branch_merge",
    )(ya, yb, wa, wb, p, p)


def _final_kernel(*refs, with_next):
    if with_next:
        x_ref, out_ref, gate_ref, gpost_ref, sh_ref, sc_ref, gpre_ref, y_ref, h_ref = refs
    else:
        x_ref, out_ref, gate_ref, gpost_ref, y_ref = refs
    out = out_ref[0]
    normed = out * lax.rsqrt(jnp.mean(out * out, axis=-1, keepdims=True) + EPS) * gpost_ref[...]
    y = x_ref[0] + gate_ref[0] * normed
    y_ref[0] = y
    if with_next:
        h_ref[0] = _modulated_norm(y, gpre_ref[...], sc_ref[0], sh_ref[0]).astype(BF16)


def _finalize(x, out, mod, g_post, next_mod, next_gain):
    b, s, d = x.shape
    ts = _largest_tile(s, (256, 128, 64, 32, 16, 8))
    with_next = next_mod is not None
    tile = pl.BlockSpec((1, ts, d), lambda bi, i: (bi, i, 0))
    row = pl.BlockSpec((1, d), lambda bi, i: (0, 0))

    def mod_spec(part):
        return pl.BlockSpec((1, 1, d), lambda bi, i, part=part: (bi, 0, part))

    in_specs = [tile, tile, mod_spec(2), row]
    args = [x, out, mod, g_post.reshape(1, d)]
    out_specs = [tile]
    out_shape = [jax.ShapeDtypeStruct((b, s, d), F32)]
    if with_next:
        in_specs += [mod_spec(0), mod_spec(1), row]
        args += [next_mod, next_mod, next_gain.reshape(1, d)]
        out_specs.append(tile)
        out_shape.append(jax.ShapeDtypeStruct((b, s, d), BF16))
    res = pl.pallas_call(
        functools.partial(_final_kernel, with_next=with_next),
        grid=(b, s // ts),
        in_specs=in_specs,
        out_specs=out_specs,
        out_shape=out_shape,
        compiler_params=_params("parallel", "parallel"),
        name="postnorm_residual",
    )(*args)
    return (res[0], res[1]) if with_next else (res[0], None)


def _rope_tables(s):
    half = LANE // 2
    inv = ROPE_THETA ** (-jnp.arange(half, dtype=F32) / half)
    ang = jnp.arange(s, dtype=F32)[:, None] * inv[None, :]
    cos, sin = jnp.cos(ang), jnp.sin(ang)
    return jnp.concatenate([cos, cos], axis=1), jnp.concatenate([-sin, sin], axis=1)


def _pad_row(v):
    v = v.reshape(1, -1).astype(F32)
    return jnp.pad(v, ((0, 0), (0, LANE - v.shape[1])))


def kernel(x_prompt, x_sample, c_prompt, c_sample, w_ada, b_ada, norm_pre, norm_post, w_in, conv_a,
           a_log, dt_bias, norm_a, w_up_a, w_up_b, w_out):
    depth, d, _ = w_in.shape
    ah = d // 256
    bh = d // 512
    aw = ah * LANE
    n_ab = 4 * ah
    assert n_ab <= LANE
    hq_b = 4 * ah
    hk_b = hq_b + 3 * bh
    hv_b = hk_b + 3 * bh
    hz_b = hv_b + 3 * bh
    hg_a = hz_b + bh
    hg_b = hg_a + d // LANE
    c0 = 4 * aw
    c1 = c0 + n_ab

    bp, bs = c_prompt.shape[0], c_sample.shape[0]
    c_all = jnp.concatenate([c_prompt, c_sample], axis=0)
    c_all = jnp.pad(c_all, ((0, -c_all.shape[0] % 8), (0, 0)))
    mod_all = _modulation(c_all, w_ada, b_ada)

    w_main = [jnp.concatenate([w_in[l, :, :c0], w_in[l, :, c1:]], axis=1).astype(BF16) for l in range(depth)]
    w_ab = [jnp.pad(w_in[l, :, c0:c1], ((0, 0), (0, LANE - n_ab))).astype(BF16) for l in range(depth)]
    wa = [w_up_a[l].astype(BF16) for l in range(depth)]
    wb = [w_up_b[l].astype(BF16) for l in range(depth)]
    wo = [w_out[l].astype(BF16) for l in range(depth)]
    perms = [_residue_permutation(dil) for _, dil in B_GROUPS if dil > 1]
    unperms = [pm.T for pm in perms]

    def trunk(x, row0, nb):
        s = x.shape[1]
        cos_t, sin_t = _rope_tables(s)
        mods = [mod_all[l, row0:row0 + nb].reshape(nb, 1, 3 * d) for l in range(depth)]
        h = _prenorm(x, mods[0], norm_pre[0])
        for l in range(depth):
            p = _inproj(h, w_main[l], cos_t, sin_t, perms, hq_b, hv_b, hz_b, bh)
            ab = _matmul(h, w_ab[l], F32, "in_proj_gates")
            gates = _gates(ab, _pad_row(a_log[l]), _pad_row(dt_bias[l]), ah)
            ya = _gdn(p, gates, conv_a[l], norm_a[l].reshape(1, LANE), ah)
            outs, lses = [], []
            for gi, (window, dilation) in enumerate(B_GROUPS):
                o, lse = _window_attention(p, hq_b + gi * bh, hk_b + gi * bh, hv_b + gi * bh, bh,
                                           window, dilation)
                outs.append(o)
                lses.append(lse)
            yb = _combine(outs, lses, unperms, p, hz_b, bh)
            merged = _merge(ya, yb, wa[l], wb[l], p, hg_a, hg_b)
            out = _matmul(merged, wo[l], F32, "out_proj")
            last = l == depth - 1
            x, h = _finalize(x, out, mods[l], norm_post[l],
                             None if last else mods[l + 1], None if last else norm_pre[l + 1])
        return x

    y_prompt = trunk(x_prompt, 0, bp)
    y_sample = trunk(x_sample, bp, bs)
    return (y_prompt, y_sample)
```

```python
import functools
import math

import jax
import jax.numpy as jnp
from jax import lax
from jax.experimental import pallas as pl
from jax.experimental.pallas import tpu as pltpu

F32 = jnp.float32
BF16 = jnp.bfloat16

LANE = 128
BF16_SUBLANES = 16
V7X_VMEM_BYTES = 64 * 1024 * 1024
VMEM_LIMIT_BYTES = V7X_VMEM_BYTES - 8 * 1024 * 1024

EPS = 1e-6
L2_EPS = 1e-6
ROPE_THETA = 10000.0
NEG_INF = -1e30
CONV_K = 5
B_GROUPS = ((128, 1), (512, 4), (2048, 16))
GDN_CHUNK = 128
GDN_CHUNKS_PER_STEP = 8
ATTN_TQ = 128
ATTN_BLOCKS_PER_STEP = 8
PERM_ROWS = 256


def _params(*sem):
    return pltpu.CompilerParams(dimension_semantics=sem, vmem_limit_bytes=VMEM_LIMIT_BYTES)


def _largest_tile(n, candidates):
    for c in candidates:
        if n % c == 0:
            return c
    raise ValueError(f"no tile in {candidates} divides {n}")


def _dot(a, b):
    return jnp.dot(a, b, preferred_element_type=F32)


def _dot_exact(a, b):
    return jnp.dot(a, b, precision=lax.Precision.HIGHEST, preferred_element_type=F32)


def _dot_nt(a, b):
    return lax.dot_general(a, b, (((1,), (1,)), ((), ())), preferred_element_type=F32)


def _silu(x):
    return x * jax.nn.sigmoid(x)


def _single_buffered(shape, index_map):
    return pl.BlockSpec(shape, index_map, pipeline_mode=pl.Buffered(1))


def _mod_kernel(c_ref, w_ref, b_ref, o_ref):
    c = c_ref[...]
    o_ref[0] = _dot(_silu(c).astype(BF16), w_ref[0].astype(BF16)) + b_ref[0]


def _modulation(c_all, w_ada, b_ada):
    depth, d, n = w_ada.shape
    r = c_all.shape[0]
    tn = _largest_tile(n, (512, 256, 128))
    return pl.pallas_call(
        _mod_kernel,
        grid=(depth, n // tn),
        in_specs=[
            pl.BlockSpec((r, d), lambda l, j: (0, 0)),
            pl.BlockSpec((1, d, tn), lambda l, j: (l, 0, j)),
            pl.BlockSpec((1, 1, tn), lambda l, j: (l, 0, j)),
        ],
        out_specs=pl.BlockSpec((1, r, tn), lambda l, j: (l, 0, j)),
        out_shape=jax.ShapeDtypeStruct((depth, r, n), F32),
        compiler_params=_params("parallel", "parallel"),
        name="adaln_mod",
    )(c_all, w_ada, b_ada.reshape(depth, 1, n))


def _modulated_norm(x, gain, scale, shift):
    y = x * lax.rsqrt(jnp.mean(x * x, axis=-1, keepdims=True) + EPS)
    return (y * gain) * (1.0 + scale) + shift


def _prenorm_kernel(x_ref, sh_ref, sc_ref, g_ref, o_ref):
    o_ref[0] = _modulated_norm(x_ref[0], g_ref[...], sc_ref[0], sh_ref[0]).astype(BF16)


def _prenorm(x, mod, gain):
    b, s, d = x.shape
    ts = _largest_tile(s, (256, 128, 64, 32, 16, 8))
    return pl.pallas_call(
        _prenorm_kernel,
        grid=(b, s // ts),
        in_specs=[
            pl.BlockSpec((1, ts, d), lambda bi, i: (bi, i, 0)),
            pl.BlockSpec((1, 1, d), lambda bi, i: (bi, 0, 0)),
            pl.BlockSpec((1, 1, d), lambda bi, i: (bi, 0, 1)),
            pl.BlockSpec((1, d), lambda bi, i: (0, 0)),
        ],
        out_specs=pl.BlockSpec((1, ts, d), lambda bi, i: (bi, i, 0)),
        out_shape=jax.ShapeDtypeStruct((b, s, d), BF16),
        compiler_params=_params("parallel", "parallel"),
        name="prenorm",
    )(x, mod, mod, gain.reshape(1, d))


def _residue_permutation(dilation):
    i = jnp.arange(PERM_ROWS)
    src = (i % (PERM_ROWS // dilation)) * dilation + i // (PERM_ROWS // dilation)
    return (src[:, None] == jnp.arange(PERM_ROWS)[None, :]).astype(BF16)


def _inproj_kernel(h_ref, w_ref, cos_ref, sin_ref, *rest, heads_per_tile, hq_b, hv_b, hz_b, bh, dilations):
    perm_refs, o_ref = rest[:-1], rest[-1]
    tm = h_ref.shape[1]
    acc = _dot(h_ref[0], w_ref[...])
    perm_of = {}
    for d in dilations:
        if d > 1:
            perm_of[d] = perm_refs[len(perm_of)]
    head0 = pl.program_id(2) * heads_per_tile
    in_b = jnp.logical_and(head0 >= hq_b, head0 < hz_b)
    group = jnp.where(in_b, ((head0 - hq_b) // bh) % len(dilations), 0)
    is_rope = jnp.logical_and(head0 >= hq_b, head0 < hv_b)
    for k in range(heads_per_tile):
        o_ref[0, k] = acc[:, k * LANE:(k + 1) * LANE].astype(BF16)
    for rope in (False, True):
        rope_matches = is_rope if rope else jnp.logical_not(is_rope)
        for g, d in enumerate(dilations):
            if not rope and d == 1:
                continue

            @pl.when(jnp.logical_and(rope_matches, group == g))
            def _(rope=rope, d=d):
                slabs = []
                for k in range(heads_per_tile):
                    slab = acc[:, k * LANE:(k + 1) * LANE]
                    if rope:
                        slab = slab * cos_ref[...] + pltpu.roll(slab, LANE // 2, 1) * sin_ref[...]
                    slabs.append(slab.astype(BF16))
                if d == 1:
                    for k in range(heads_per_tile):
                        o_ref[0, k] = slabs[k]
                else:
                    tile = jnp.concatenate(slabs, axis=1)
                    for s in range(tm // PERM_ROWS):
                        rows = slice(s * PERM_ROWS, (s + 1) * PERM_ROWS)
                        moved = _dot(perm_of[d][...], tile[rows, :]).astype(BF16)
                        for k in range(heads_per_tile):
                            o_ref[0, k, rows, :] = moved[:, k * LANE:(k + 1) * LANE]


def _inproj(h, w, cos_t, sin_t, perms, hq_b, hv_b, hz_b, bh):
    b, s, d = h.shape
    n = w.shape[1]
    tm = _largest_tile(s, (1024, 512, 256))
    tn = _largest_tile(math.gcd(n, bh * LANE), (1024, 512, 256, 128))
    hpt = tn // LANE
    assert hq_b % hpt == 0
    dilations = tuple(dil for _, dil in B_GROUPS)
    kern = functools.partial(_inproj_kernel, heads_per_tile=hpt, hq_b=hq_b, hv_b=hv_b, hz_b=hz_b, bh=bh,
                             dilations=dilations)
    perm_spec = pl.BlockSpec((PERM_ROWS, PERM_ROWS), lambda bi, i, j: (0, 0))
    return pl.pallas_call(
        kern,
        grid=(b, s // tm, n // tn),
        in_specs=[
            pl.BlockSpec((1, tm, d), lambda bi, i, j: (bi, i, 0)),
            pl.BlockSpec((d, tn), lambda bi, i, j: (0, j)),
            pl.BlockSpec((tm, LANE), lambda bi, i, j: (i, 0)),
            pl.BlockSpec((tm, LANE), lambda bi, i, j: (i, 0)),
        ] + [perm_spec] * len(perms),
        out_specs=pl.BlockSpec((1, hpt, tm, LANE), lambda bi, i, j: (bi, j, i, 0)),
        out_shape=jax.ShapeDtypeStruct((b, n // LANE, s, LANE), BF16),
        compiler_params=_params("parallel", "parallel", "arbitrary"),
        name="in_proj",
    )(h, w, cos_t, sin_t, *perms)


def _mm_kernel(x_ref, w_ref, o_ref):
    o_ref[0] = _dot(x_ref[0], w_ref[...]).astype(o_ref.dtype)


def _matmul(x, w, out_dtype, name):
    b, s, k = x.shape
    n = w.shape[1]
    tm = _largest_tile(s, (1024, 512, 256, 128))
    tn = _largest_tile(n, (1024, 512, 256, 128))
    return pl.pallas_call(
        _mm_kernel,
        grid=(b, s // tm, n // tn),
        in_specs=[
            pl.BlockSpec((1, tm, k), lambda bi, i, j: (bi, i, 0)),
            pl.BlockSpec((k, tn), lambda bi, i, j: (0, j)),
        ],
        out_specs=pl.BlockSpec((1, tm, tn), lambda bi, i, j: (bi, i, j)),
        out_shape=jax.ShapeDtypeStruct((b, s, n), out_dtype),
        compiler_params=_params("parallel", "parallel", "arbitrary"),
        name=name,
    )(x, w)


def _gate_kernel(ab_ref, alog_ref, dtb_ref, o_ref, *, chunk, heads, tiles):
    C, H = chunk, heads
    row = lax.broadcasted_iota(jnp.int32, (C, C), 0)
    col = lax.broadcasted_iota(jnp.int32, (C, C), 1)
    lane = lax.broadcasted_iota(jnp.int32, (1, LANE), 1)
    lower = (row >= col).astype(F32)
    upper = (row <= col).astype(F32)
    for t in range(tiles):
        rows = slice(t * C, (t + 1) * C)
        ab = ab_ref[0, rows, :]
        gval = -jnp.exp(alog_ref[...]) * jax.nn.softplus(ab + dtb_ref[...])
        prefix = _dot_exact(lower, gval)
        suffix = _dot_exact(upper, gval)
        o_ref[0, rows, :] = jnp.where(lane < H, prefix, jnp.where(lane < 2 * H, suffix, jax.nn.sigmoid(ab)))


def _gates(ab, alog_row, dtb_row, heads):
    b, s, _ = ab.shape
    c = GDN_CHUNK
    tiles = 2 if s % (2 * c) == 0 else 1
    ts = tiles * c
    row_spec = pl.BlockSpec((1, LANE), lambda bi, i: (0, 0))
    tile = pl.BlockSpec((1, ts, LANE), lambda bi, i: (bi, i, 0))
    return pl.pallas_call(
        functools.partial(_gate_kernel, chunk=c, heads=heads, tiles=tiles),
        grid=(b, s // ts),
        in_specs=[tile, row_spec, row_spec],
        out_specs=tile,
        out_shape=jax.ShapeDtypeStruct((b, s, LANE), F32),
        compiler_params=_params("parallel", "parallel"),
        name="deltanet_gates",
    )(ab, alog_row, dtb_row)


def _tri_inverse_stages(lmats, n, result):
    row = lax.broadcasted_iota(jnp.int32, (n, n), 0)
    col = lax.broadcasted_iota(jnp.int32, (n, n), 1)
    eye = jnp.where(row == col, 1.0, 0.0)
    pair = (row >> 1) == (col >> 1)
    ts = [eye - jnp.where(pair, lm, 0.0) for lm in lmats]
    for lv in range(1, int(math.log2(n))):
        join = jnp.logical_and((row >> (lv + 1)) == (col >> (lv + 1)), (row >> lv) != (col >> lv))
        tbs = [t.astype(BF16) for t in ts]
        mids = [_dot(tb, jnp.where(join, lm, 0.0).astype(BF16)) for tb, lm in zip(tbs, lmats)]
        yield
        ts = [t - _dot(mid.astype(BF16), tb) for t, mid, tb in zip(ts, mids, tbs)]
        yield
    result.extend(ts)


def _emit_interleaved(generators):
    active = list(generators)
    while active:
        for gen in list(active):
            try:
                next(gen)
            except StopIteration:
                active.remove(gen)


def _gdn_kernel(q_ref, k_ref, v_ref, z_ref, gate_ref, cq_ref, ck_ref, cv_ref, nw_ref, o_ref,
                oacc, cstage, qn, kn, vn, us, ws, qds, kdt, att, gls, *, seq, chunk, heads):
    S, C, H = seq, chunk, heads
    nc = S // C
    G = GDN_CHUNKS_PER_STEP
    R = C
    nr = S // R
    halo = BF16_SUBLANES
    h = pl.program_id(1)
    lane = lax.broadcasted_iota(jnp.int32, (1, LANE), 1)
    row = lax.broadcasted_iota(jnp.int32, (C, C), 0)
    col = lax.broadcasted_iota(jnp.int32, (C, C), 1)

    def conv_tile(slot, src_ref, cw_ref, dst_ref, norm_scale, r):
        r0 = pl.multiple_of(r * R, R)
        prev0 = pl.multiple_of(jnp.maximum(r0 - halo, 0), halo)
        next0 = pl.multiple_of(jnp.minimum(r0 + R, S - halo), halo)
        prev = src_ref[0, 0, pl.ds(prev0, halo), :].astype(F32)
        nxt = src_ref[0, 0, pl.ds(next0, halo), :].astype(F32)
        cstage[slot, 0:halo, :] = jnp.where(r > 0, prev, 0.0)
        cstage[slot, halo:halo + R, :] = src_ref[0, 0, pl.ds(r0, R), :].astype(F32)
        cstage[slot, halo + R:, :] = jnp.where(r < nr - 1, nxt, 0.0)
        y = jnp.zeros((R, LANE), F32)
        for j in range(CONV_K):
            off = halo - CONV_K // 2 + j
            y = y + cw_ref[j:j + 1, :] * cstage[slot, off:off + R, :]
        y = _silu(y)
        if norm_scale is not None:
            y = y * (lax.rsqrt(jnp.sum(y * y, axis=-1, keepdims=True) + L2_EPS) * norm_scale)
        dst_ref[pl.ds(r0, R), :] = y.astype(dst_ref.dtype)

    def conv_body(r, carry):
        conv_tile(0, q_ref, cq_ref, qn, LANE ** -0.5, r)
        conv_tile(1, k_ref, ck_ref, kn, 1.0, r)
        conv_tile(2, v_ref, cv_ref, vn, None, r)
        oacc[pl.ds(pl.multiple_of(r * R, R), R), :] = jnp.zeros((R, LANE), F32)
        return carry

    lax.fori_loop(0, nr, conv_body, 0)

    ng = nc // G
    masks = [((row >= col), (row > col)), ((row <= col), (row < col))]

    def chain_chunks(i):
        return [(0, g, i * G + g) for g in range(G)] + [(1, g, nc - (i + 1) * G + g) for g in range(G)]

    def prepare_stages(i, slot):
        items = []
        for dirn, g, c in chain_chunks(i):
            c0 = pl.multiple_of(c * C, C)
            items.append((dirn, g, qn[pl.ds(c0, C), :], kn[pl.ds(c0, C), :],
                          vn[pl.ds(c0, C), :].astype(F32), gate_ref[0, pl.ds(c0, C), :]))
        kks = [_dot_nt(kb, kb) for (_, _, _, kb, _, _) in items]
        qks = [_dot_nt(qb, kb) for (_, _, qb, kb, _, _) in items]
        yield
        chains = []
        for (dirn, g, qb, kb, vc, gt), kk, qk in zip(items, kks, qks):
            incl, strict = masks[dirn]
            gc = jnp.sum(jnp.where(lane == dirn * H + h, gt, 0.0), axis=1, keepdims=True)
            beta = jnp.sum(jnp.where(lane == (2 + dirn) * H + h, gt, 0.0), axis=1, keepdims=True)
            gcb = jnp.broadcast_to(gc, (C, C))
            decay = jnp.exp(jnp.where(incl, gcb - gcb.T, NEG_INF))
            lmat = jnp.where(strict, (beta * kk) * decay, 0.0)
            amat = jnp.where(incl, qk * decay, 0.0)
            chains.append((dirn, g, qb, kb, vc, gc, beta, amat, lmat))
        tinvs = []
        yield from _tri_inverse_stages([ch[8] for ch in chains], C, tinvs)
        rhss = []
        for (dirn, g, qb, kb, vc, gc, beta, amat, lmat) in chains:
            rhs = jnp.concatenate([vc * beta, kb.astype(F32) * (beta * jnp.exp(gc))], axis=1)
            rhss.append(rhs.astype(BF16))
        uws = [_dot(tinv.astype(BF16), rhs) for tinv, rhs in zip(tinvs, rhss)]
        yield
        for (dirn, g, qb, kb, vc, gc, beta, amat, lmat), uw in zip(chains, uws):
            g_last = gc[C - 1:C, :] if dirn == 0 else gc[0:1, :]
            kd = kb.astype(F32) * jnp.exp(g_last - gc)
            rows = slice(g * C, (g + 1) * C)
            us[slot, dirn, rows, :] = uw[:, :LANE].astype(BF16)
            ws[slot, dirn, rows, :] = uw[:, LANE:].astype(BF16)
            qds[slot, dirn, rows, :] = (qb.astype(F32) * jnp.exp(gc)).astype(BF16)
            kdt[slot, dirn, g * LANE:(g + 1) * LANE, :] = kd.T.astype(BF16)
            att[slot, dirn, rows, :] = amat.astype(BF16)
            gls[slot, dirn, g:g + 1, :] = jnp.broadcast_to(jnp.exp(g_last), (1, LANE))

    def update_stages(i, slot, states, result):
        states = list(states)
        for step in range(G):
            gs = (step, G - 1 - step)
            cs = (i * G + gs[0], nc - (i + 1) * G + gs[1])
            rows = [slice(g * C, (g + 1) * C) for g in gs]
            sbs = [st.astype(BF16) for st in states]
            wss = [_dot(ws[slot, d, rows[d], :], sbs[d]) for d in (0, 1)]
            qss = [_dot(qds[slot, d, rows[d], :], sbs[d]) for d in (0, 1)]
            yield
            vbs = [(us[slot, d, rows[d], :].astype(F32) - wss[d]).astype(BF16) for d in (0, 1)]
            avs = [_dot(att[slot, d, rows[d], :], vbs[d]) for d in (0, 1)]
            kvs = [_dot(kdt[slot, d, gs[d] * LANE:(gs[d] + 1) * LANE, :], vbs[d]) for d in (0, 1)]
            yield
            for d in (0, 1):
                out_rows = pl.ds(pl.multiple_of(cs[d] * C, C), C)
                oacc[out_rows, :] = oacc[out_rows, :] + (qss[d] + avs[d])
            states = [states[d] * gls[slot, d, gs[d]:gs[d] + 1, :] + kvs[d] for d in (0, 1)]
        result.extend(states)

    _emit_interleaved([prepare_stages(0, 0)])

    def pipelined_body(i, states):
        new_states = []
        _emit_interleaved([prepare_stages(i, i % 2), update_stages(i - 1, (i - 1) % 2, states, new_states)])
        return tuple(new_states)

    zero_state = jnp.zeros((LANE, LANE), F32)
    states = lax.fori_loop(1, ng, pipelined_body, (zero_state, zero_state))
    _emit_interleaved([update_stages(ng - 1, (ng - 1) % 2, states, [])])

    def finish_tile(r):
        r0 = pl.multiple_of(r * R, R)
        o = oacc[pl.ds(r0, R), :]
        y = o * lax.rsqrt(jnp.mean(o * o, axis=-1, keepdims=True) + EPS) * nw_ref[...]
        z = z_ref[0, 0, pl.ds(r0, R), :].astype(F32)
        o_ref[0, pl.ds(r0, R), :] = (y * _silu(z)).astype(BF16)

    def finish_body(i, carry):
        finish_tile(2 * i)
        finish_tile(2 * i + 1)
        return carry

    lax.fori_loop(0, nr // 2, finish_body, 0)


def _gdn(p, gates, conv_w, norm_w, heads):
    b, _, s, _ = p.shape
    c = GDN_CHUNK
    nc = s // c
    g = GDN_CHUNKS_PER_STEP
    assert s % (2 * c) == 0 and nc % (2 * g) == 0
    hq, hk, hv, hz = 0, heads, 2 * heads, 3 * heads

    def head_spec(off):
        return pl.BlockSpec((1, 1, s, LANE), lambda bi, hi, off=off: (bi, off + hi, 0, 0))

    def conv_spec(off):
        return pl.BlockSpec((CONV_K, LANE), lambda bi, hi, off=off: (0, off + hi))

    kern = functools.partial(_gdn_kernel, seq=s, chunk=c, heads=heads)
    return pl.pallas_call(
        kern,
        grid=(b, heads),
        in_specs=[
            head_spec(hq), head_spec(hk), head_spec(hv), head_spec(hz),
            _single_buffered((1, s, LANE), lambda bi, hi: (bi, 0, 0)),
            conv_spec(hq), conv_spec(hk), conv_spec(hv),
            pl.BlockSpec((1, LANE), lambda bi, hi: (0, 0)),
        ],
        out_specs=pl.BlockSpec((1, s, LANE), lambda bi, hi: (bi, 0, hi)),
        out_shape=jax.ShapeDtypeStruct((b, s, heads * LANE), BF16),
        scratch_shapes=[
            pltpu.VMEM((s, LANE), F32),
            pltpu.VMEM((3, c + 2 * BF16_SUBLANES, LANE), F32),
            pltpu.VMEM((s, LANE), BF16),
            pltpu.VMEM((s, LANE), BF16),
            pltpu.VMEM((s, LANE), BF16),
            pltpu.VMEM((2, 2, g * c, LANE), BF16),
            pltpu.VMEM((2, 2, g * c, LANE), BF16),
            pltpu.VMEM((2, 2, g * c, LANE), BF16),
            pltpu.VMEM((2, 2, g * LANE, c), BF16),
            pltpu.VMEM((2, 2, g * c, c), BF16),
            pltpu.VMEM((2, 2, max(g, 8), LANE), F32),
        ],
        compiler_params=_params("parallel", "arbitrary"),
        name="gated_deltanet",
    )(p, p, p, p, gates, conv_w, conv_w, conv_w, norm_w)


def _attend_many(blocks, reach):
    scores = [_dot_nt(q, k) * (LANE ** -0.5) for (q, k, _, _, _) in blocks]
    probs, stats = [], []
    for s, (q, k, _, q0, k0) in zip(scores, blocks):
        tq, w = q.shape[0], k.shape[0]
        qpos = q0 + lax.broadcasted_iota(jnp.int32, (tq, w), 0)
        kpos = k0 + lax.broadcasted_iota(jnp.int32, (tq, w), 1)
        s = jnp.where(jnp.abs(kpos - qpos) <= reach, s, NEG_INF)
        m = jnp.max(s, axis=1, keepdims=True)
        p = jnp.exp(s - m)
        den = jnp.sum(p, axis=1, keepdims=True)
        probs.append(p.astype(BF16))
        stats.append((m, den))
    outs = [_dot(p, v) for p, (_, _, v, _, _) in zip(probs, blocks)]
    return [((o / den).astype(BF16), jnp.broadcast_to(m + jnp.log(den), (o.shape[0], LANE)))
            for o, (m, den) in zip(outs, stats)]


def _attn_kernel(q_ref, k_ref, v_ref, o_ref, lse_ref, *, length, reach, rows, dilation, unroll):
    L, R, TQ, U = length, reach, ATTN_TQ, unroll
    W = TQ + 2 * R

    def residue(r, carry):
        def body(step, c):
            blocks = []
            for u in range(U):
                i = step * U + u
                q0 = pl.multiple_of(i * TQ, TQ)
                k0 = pl.multiple_of(jnp.clip(q0 - R, 0, L - W), R)
                if rows:
                    qg = pl.ds(pl.multiple_of(q0 // rows, TQ // rows), TQ // rows)
                    kg = pl.ds(pl.multiple_of(k0 // rows, R // rows), W // rows)
                    blocks.append((q_ref[0, 0, qg, r].reshape(TQ, LANE), k_ref[0, 0, kg, r].reshape(W, LANE),
                                   v_ref[0, 0, kg, r].reshape(W, LANE), q0, k0))
                else:
                    blocks.append((q_ref[0, 0, pl.ds(q0, TQ), :], k_ref[0, 0, pl.ds(k0, W), :],
                                   v_ref[0, 0, pl.ds(k0, W), :], q0, k0))
            for u, (o, lse) in enumerate(_attend_many(blocks, R)):
                i = step * U + u
                if rows:
                    for j in range(TQ // rows):
                        part = slice(j * rows, (j + 1) * rows)
                        o_ref[0, 0, i * (TQ // rows) + j, r] = o[part, :]
                        lse_ref[0, 0, i * (TQ // rows) + j, r] = lse[part, :]
                else:
                    q0 = pl.multiple_of(i * TQ, TQ)
                    o_ref[0, 0, pl.ds(q0, TQ), :] = o
                    lse_ref[0, 0, pl.ds(q0, TQ), :] = lse
            return c

        lax.fori_loop(0, L // TQ // U, body, 0)
        return carry

    if rows:
        lax.fori_loop(0, dilation, residue, 0)
    else:
        residue(0, 0)


def _window_attention(p, hq, hk, hv, heads, window, dilation):
    b, nh, s, _ = p.shape
    reach = window // (2 * dilation)
    length = s // dilation
    unroll = math.gcd(ATTN_BLOCKS_PER_STEP, length // ATTN_TQ)
    assert length % ATTN_TQ == 0 and length >= ATTN_TQ + 2 * reach
    if dilation == 1:
        rows = 0
        pv = p
        out_dims = (b, heads, s, LANE)
        block = (1, 1, s, LANE)
        tail = (0, 0)
    else:
        rows = PERM_ROWS // dilation
        assert rows % BF16_SUBLANES == 0 and ATTN_TQ % rows == 0 and reach % rows == 0 and s % PERM_ROWS == 0
        ng = s // PERM_ROWS
        pv = p.reshape(b, nh, ng, dilation, rows, LANE)
        out_dims = (b, heads, ng, dilation, rows, LANE)
        block = (1, 1, ng, dilation, rows, LANE)
        tail = (0, 0, 0, 0)

    def spec(off):
        return pl.BlockSpec(block, lambda bi, hi, off=off: (bi, off + hi) + tail)

    kern = functools.partial(_attn_kernel, length=length, reach=reach, rows=rows, dilation=dilation,
                             unroll=unroll)
    o, lse = pl.pallas_call(
        kern,
        grid=(b, heads),
        in_specs=[spec(hq), spec(hk), spec(hv)],
        out_specs=[spec(0), spec(0)],
        out_shape=[jax.ShapeDtypeStruct(out_dims, BF16), jax.ShapeDtypeStruct(out_dims, F32)],
        compiler_params=_params("parallel", "parallel"),
        name=f"window_attention_d{dilation}",
    )(pv, pv, pv)
    return o.reshape(b, heads, s, LANE), lse.reshape(b, heads, s, LANE)


def _select_rows_f32(sel, x):
    hi = x.astype(BF16)
    rest = x - hi.astype(F32)
    mid = rest.astype(BF16)
    lo = (rest - mid.astype(F32)).astype(BF16)
    return _dot(sel, hi) + _dot(sel, mid) + _dot(sel, lo)


def _combine_kernel(*refs, heads, dilations):
    ng = len(dilations)
    o_refs, l_refs = refs[:ng], refs[ng:2 * ng]
    z_ref = refs[2 * ng]
    unperm_refs = refs[2 * ng + 1:-1]
    y_ref = refs[-1]
    unperm_of = {}
    for d in dilations:
        if d > 1:
            unperm_of[d] = unperm_refs[len(unperm_of)]
    for h in range(heads):
        outs, lses = [], []
        for g, d in enumerate(dilations):
            o, lse = o_refs[g][0, h], l_refs[g][0, h]
            if d > 1:
                o = _dot(unperm_of[d][...], o)
                lse = _select_rows_f32(unperm_of[d][...], lse)
            else:
                o = o.astype(F32)
            outs.append(o)
            lses.append(lse)
        m = functools.reduce(jnp.maximum, lses)
        wts = [jnp.exp(l - m) for l in lses]
        den = functools.reduce(lambda a, c: a + c, wts)
        acc = functools.reduce(lambda a, c: a + c, [w * o for w, o in zip(wts, outs)])
        z = z_ref[0, h].astype(F32)
        y_ref[0, :, h * LANE:(h + 1) * LANE] = (acc / den * _silu(z)).astype(BF16)


def _combine(outs, lses, unperms, p, hz, heads):
    b, _, s, _ = p.shape
    ts = PERM_ROWS
    assert hz % heads == 0 and s % ts == 0
    dilations = tuple(dil for _, dil in B_GROUPS)
    spec = pl.BlockSpec((1, heads, ts, LANE), lambda bi, i: (bi, 0, i, 0))
    n = len(outs)
    return pl.pallas_call(
        functools.partial(_combine_kernel, heads=heads, dilations=dilations),
        grid=(b, s // ts),
        in_specs=[spec] * (2 * n)
        + [pl.BlockSpec((1, heads, ts, LANE), lambda bi, i: (bi, hz // heads, i, 0))]
        + [pl.BlockSpec((PERM_ROWS, PERM_ROWS), lambda bi, i: (0, 0))] * len(unperms),
        out_specs=pl.BlockSpec((1, ts, heads * LANE), lambda bi, i: (bi, i, 0)),
        out_shape=jax.ShapeDtypeStruct((b, s, heads * LANE), BF16),
        compiler_params=_params("parallel", "parallel"),
        name="group_combine",
    )(*outs, *lses, p, *unperms)


def _merge_kernel(ya_ref, yb_ref, wa_ref, wb_ref, ga_ref, gb_ref, o_ref, *, heads_per_tile):
    acc_a = _dot(ya_ref[0], wa_ref[...])
    acc_b = _dot(yb_ref[0], wb_ref[...])
    for k in range(heads_per_tile):
        sl = slice(k * LANE, (k + 1) * LANE)
        ga = jax.nn.sigmoid(ga_ref[0, k].astype(F32))
        gb = jax.nn.sigmoid(gb_ref[0, k].astype(F32))
        o_ref[0, :, sl] = (ga * acc_a[:, sl] + gb * acc_b[:, sl]).astype(BF16)


def _merge(ya, yb, wa, wb, p, hga, hgb):
    b, s, ka = ya.shape
    kb = yb.shape[2]
    d = wa.shape[1]
    tm = _largest_tile(s, (1024, 512, 256, 128))
    tn = _largest_tile(math.gcd(d, hga * LANE, hgb * LANE), (1024, 512, 256, 128))
    hpt = tn // LANE
    return pl.pallas_call(
        functools.partial(_merge_kernel, heads_per_tile=hpt),
        grid=(b, s // tm, d // tn),
        in_specs=[
            pl.BlockSpec((1, tm, ka), lambda bi, i, j: (bi, i, 0)),
            pl.BlockSpec((1, tm, kb), lambda bi, i, j: (bi, i, 0)),
            pl.BlockSpec((ka, tn), lambda bi, i, j: (0, j)),
            pl.BlockSpec((kb, tn), lambda bi, i, j: (0, j)),
            pl.BlockSpec((1, hpt, tm, LANE), lambda bi, i, j: (bi, hga // hpt + j, i, 0)),
            pl.BlockSpec((1, hpt, tm, LANE), lambda bi, i, j: (bi, hgb // hpt + j, i, 0)),
        ],
        out_specs=pl.BlockSpec((1, tm, tn), lambda bi, i, j: (bi, i, j)),
        out_shape=jax.ShapeDtypeStruct((b, s, d), BF16),
        compiler_params=_params("parallel", "parallel", "arbitrary"),
        name="branch_merge",
    )(ya, yb, wa, wb, p, p)


def _final_kernel(*refs, with_next):
    if with_next:
        x_ref, out_ref, gate_ref, gpost_ref, sh_ref, sc_ref, gpre_ref, y_ref, h_ref = refs
    else:
        x_ref, out_ref, gate_ref, gpost_ref, y_ref = refs
    out = out_ref[0]
    normed = out * lax.rsqrt(jnp.mean(out * out, axis=-1, keepdims=True) + EPS) * gpost_ref[...]
    y = x_ref[0] + gate_ref[0] * normed
    y_ref[0] = y
    if with_next:
        h_ref[0] = _modulated_norm(y, gpre_ref[...], sc_ref[0], sh_ref[0]).astype(BF16)


def _finalize(x, out, mod, g_post, next_mod, next_gain):
    b, s, d = x.shape
    ts = _largest_tile(s, (256, 128, 64, 32, 16, 8))
    with_next = next_mod is not None
    tile = pl.BlockSpec((1, ts, d), lambda bi, i: (bi, i, 0))
    row = pl.BlockSpec((1, d), lambda bi, i: (0, 0))

    def mod_spec(part):
        return pl.BlockSpec((1, 1, d), lambda bi, i, part=part: (bi, 0, part))

    in_specs = [tile, tile, mod_spec(2), row]
    args = [x, out, mod, g_post.reshape(1, d)]
    out_specs = [tile]
    out_shape = [jax.ShapeDtypeStruct((b, s, d), F32)]
    if with_next:
        in_specs += [mod_spec(0), mod_spec(1), row]
        args += [next_mod, next_mod, next_gain.reshape(1, d)]
        out_specs.append(tile)
        out_shape.append(jax.ShapeDtypeStruct((b, s, d), BF16))
    res = pl.pallas_call(
        functools.partial(_final_kernel, with_next=with_next),
        grid=(b, s // ts),
        in_specs=in_specs,
        out_specs=out_specs,
        out_shape=out_shape,
        compiler_params=_params("parallel", "parallel"),
        name="postnorm_residual",
    )(*args)
    return (res[0], res[1]) if with_next else (res[0], None)


def _rope_tables(s):
    half = LANE // 2
    inv = ROPE_THETA ** (-jnp.arange(half, dtype=F32) / half)
    ang = jnp.arange(s, dtype=F32)[:, None] * inv[None, :]
    cos, sin = jnp.cos(ang), jnp.sin(ang)
    return jnp.concatenate([cos, cos], axis=1), jnp.concatenate([-sin, sin], axis=1)


def _pad_row(v):
    v = v.reshape(1, -1).astype(F32)
    return jnp.pad(v, ((0, 0), (0, LANE - v.shape[1])))


def kernel(x_prompt, x_sample, c_prompt, c_sample, w_ada, b_ada, norm_pre, norm_post, w_in, conv_a,
           a_log, dt_bias, norm_a, w_up_a, w_up_b, w_out):
    depth, d, _ = w_in.shape
    ah = d // 256
    bh = d // 512
    aw = ah * LANE
    n_ab = 4 * ah
    assert n_ab <= LANE
    hq_b = 4 * ah
    hk_b = hq_b + 3 * bh
    hv_b = hk_b + 3 * bh
    hz_b = hv_b + 3 * bh
    hg_a = hz_b + bh
    hg_b = hg_a + d // LANE
    c0 = 4 * aw
    c1 = c0 + n_ab

    bp, bs = c_prompt.shape[0], c_sample.shape[0]
    c_all = jnp.concatenate([c_prompt, c_sample], axis=0)
    c_all = jnp.pad(c_all, ((0, -c_all.shape[0] % 8), (0, 0)))
    mod_all = _modulation(c_all, w_ada, b_ada)

    w_main = [jnp.concatenate([w_in[l, :, :c0], w_in[l, :, c1:]], axis=1).astype(BF16) for l in range(depth)]
    w_ab = [jnp.pad(w_in[l, :, c0:c1], ((0, 0), (0, LANE - n_ab))).astype(BF16) for l in range(depth)]
    wa = [w_up_a[l].astype(BF16) for l in range(depth)]
    wb = [w_up_b[l].astype(BF16) for l in range(depth)]
    wo = [w_out[l].astype(BF16) for l in range(depth)]
    perms = [_residue_permutation(dil) for _, dil in B_GROUPS if dil > 1]
    unperms = [pm.T for pm in perms]

    def trunk(x, row0, nb):
        s = x.shape[1]
        cos_t, sin_t = _rope_tables(s)
        mods = [mod_all[l, row0:row0 + nb].reshape(nb, 1, 3 * d) for l in range(depth)]
        h = _prenorm(x, mods[0], norm_pre[0])
        for l in range(depth):
            p = _inproj(h, w_main[l], cos_t, sin_t, perms, hq_b, hv_b, hz_b, bh)
            ab = _matmul(h, w_ab[l], F32, "in_proj_gates")
            gates = _gates(ab, _pad_row(a_log[l]), _pad_row(dt_bias[l]), ah)
            ya = _gdn(p, gates, conv_a[l], norm_a[l].reshape(1, LANE), ah)
            outs, lses = [], []
            for gi, (window, dilation) in enumerate(B_GROUPS):
                o, lse = _window_attention(p, hq_b + gi * bh, hk_b + gi * bh, hv_b + gi * bh, bh,
                                           window, dilation)
                outs.append(o)
                lses.append(lse)
            yb = _combine(outs, lses, unperms, p, hz_b, bh)
            merged = _merge(ya, yb, wa[l], wb[l], p, hg_a, hg_b)
            out = _matmul(merged, wo[l], F32, "out_proj")
            last = l == depth - 1
            x, h = _finalize(x, out, mods[l], norm_post[l],
                             None if last else mods[l + 1], None if last else norm_pre[l + 1])
        return x

    y_prompt = trunk(x_prompt, 0, bp)
    y_sample = trunk(x_sample, bp, bs)
    return (y_prompt, y_sample)
```

```python
import functools
import math

import jax
import jax.numpy as jnp
from jax import lax
from jax.experimental import pallas as pl
from jax.experimental.pallas import tpu as pltpu

F32 = jnp.float32
BF16 = jnp.bfloat16

LANE = 128
BF16_SUBLANES = 16
V7X_VMEM_BYTES = 64 * 1024 * 1024
VMEM_LIMIT_BYTES = V7X_VMEM_BYTES - 8 * 1024 * 1024

EPS = 1e-6
L2_EPS = 1e-6
ROPE_THETA = 10000.0
NEG_INF = -1e30
CONV_K = 5
B_GROUPS = ((128, 1), (512, 4), (2048, 16))
GDN_CHUNK = 128
GDN_CHUNKS_PER_STEP = 8
ATTN_TQ = 128
ATTN_BLOCKS_PER_STEP = 8
PERM_ROWS = 256


def _params(*sem):
    return pltpu.CompilerParams(dimension_semantics=sem, vmem_limit_bytes=VMEM_LIMIT_BYTES)


def _largest_tile(n, candidates):
    for c in candidates:
        if n % c == 0:
            return c
    raise ValueError(f"no tile in {candidates} divides {n}")


def _dot(a, b):
    return jnp.dot(a, b, preferred_element_type=F32)


def _dot_exact(a, b):
    return jnp.dot(a, b, precision=lax.Precision.HIGHEST, preferred_element_type=F32)


def _dot_nt(a, b):
    return lax.dot_general(a, b, (((1,), (1,)), ((), ())), preferred_element_type=F32)


def _silu(x):
    return x * jax.nn.sigmoid(x)


def _single_buffered(shape, index_map):
    return pl.BlockSpec(shape, index_map, pipeline_mode=pl.Buffered(1))


def _mod_kernel(c_ref, w_ref, b_ref, o_ref):
    c = c_ref[...]
    o_ref[0] = _dot(_silu(c).astype(BF16), w_ref[0].astype(BF16)) + b_ref[0]


def _modulation(c_all, w_ada, b_ada):
    depth, d, n = w_ada.shape
    r = c_all.shape[0]
    tn = _largest_tile(n, (512, 256, 128))
    return pl.pallas_call(
        _mod_kernel,
        grid=(depth, n // tn),
        in_specs=[
            pl.BlockSpec((r, d), lambda l, j: (0, 0)),
            pl.BlockSpec((1, d, tn), lambda l, j: (l, 0, j)),
            pl.BlockSpec((1, 1, tn), lambda l, j: (l, 0, j)),
        ],
        out_specs=pl.BlockSpec((1, r, tn), lambda l, j: (l, 0, j)),
        out_shape=jax.ShapeDtypeStruct((depth, r, n), F32),
        compiler_params=_params("parallel", "parallel"),
        name="adaln_mod",
    )(c_all, w_ada, b_ada.reshape(depth, 1, n))


def _modulated_norm(x, gain, scale, shift):
    y = x * lax.rsqrt(jnp.mean(x * x, axis=-1, keepdims=True) + EPS)
    return (y * gain) * (1.0 + scale) + shift


def _prenorm_kernel(x_ref, sh_ref, sc_ref, g_ref, o_ref):
    o_ref[0] = _modulated_norm(x_ref[0], g_ref[...], sc_ref[0], sh_ref[0]).astype(BF16)


def _prenorm(x, mod, gain):
    b, s, d = x.shape
    ts = _largest_tile(s, (256, 128, 64, 32, 16, 8))
    return pl.pallas_call(
        _prenorm_kernel,
        grid=(b, s // ts),
        in_specs=[
            pl.BlockSpec((1, ts, d), lambda bi, i: (bi, i, 0)),
            pl.BlockSpec((1, 1, d), lambda bi, i: (bi, 0, 0)),
            pl.BlockSpec((1, 1, d), lambda bi, i: (bi, 0, 1)),
            pl.BlockSpec((1, d), lambda bi, i: (0, 0)),
        ],
        out_specs=pl.BlockSpec((1, ts, d), lambda bi, i: (bi, i, 0)),
        out_shape=jax.ShapeDtypeStruct((b, s, d), BF16),
        compiler_params=_params("parallel", "parallel"),
        name="prenorm",
    )(x, mod, mod, gain.reshape(1, d))


def _residue_permutation(dilation):
    i = jnp.arange(PERM_ROWS)
    src = (i % (PERM_ROWS // dilation)) * dilation + i // (PERM_ROWS // dilation)
    return (src[:, None] == jnp.arange(PERM_ROWS)[None, :]).astype(BF16)


def _inproj_kernel(h_ref, w_ref, cos_ref, sin_ref, *rest, heads_per_tile, hq_b, hv_b, hz_b, bh, dilations):
    perm_refs, o_ref = rest[:-1], rest[-1]
    tm = h_ref.shape[1]
    acc = _dot(h_ref[0], w_ref[...])
    perm_of = {}
    for d in dilations:
        if d > 1:
            perm_of[d] = perm_refs[len(perm_of)]
    head0 = pl.program_id(2) * heads_per_tile
    in_b = jnp.logical_and(head0 >= hq_b, head0 < hz_b)
    group = jnp.where(in_b, ((head0 - hq_b) // bh) % len(dilations), 0)
    is_rope = jnp.logical_and(head0 >= hq_b, head0 < hv_b)
    for k in range(heads_per_tile):
        o_ref[0, k] = acc[:, k * LANE:(k + 1) * LANE].astype(BF16)
    for rope in (False, True):
        rope_matches = is_rope if rope else jnp.logical_not(is_rope)
        for g, d in enumerate(dilations):
            if not rope and d == 1:
                continue

            @pl.when(jnp.logical_and(rope_matches, group == g))
            def _(rope=rope, d=d):
                slabs = []
                for k in range(heads_per_tile):
                    slab = acc[:, k * LANE:(k + 1) * LANE]
                    if rope:
                        slab = slab * cos_ref[...] + pltpu.roll(slab, LANE // 2, 1) * sin_ref[...]
                    slabs.append(slab.astype(BF16))
                if d == 1:
                    for k in range(heads_per_tile):
                        o_ref[0, k] = slabs[k]
                else:
                    tile = jnp.concatenate(slabs, axis=1)
                    for s in range(tm // PERM_ROWS):
                        rows = slice(s * PERM_ROWS, (s + 1) * PERM_ROWS)
                        moved = _dot(perm_of[d][...], tile[rows, :]).astype(BF16)
                        for k in range(heads_per_tile):
                            o_ref[0, k, rows, :] = moved[:, k * LANE:(k + 1) * LANE]


def _inproj(h, w, cos_t, sin_t, perms, hq_b, hv_b, hz_b, bh):
    b, s, d = h.shape
    n = w.shape[1]
    tm = _largest_tile(s, (1024, 512, 256))
    tn = _largest_tile(math.gcd(n, bh * LANE), (1024, 512, 256, 128))
    hpt = tn // LANE
    assert hq_b % hpt == 0
    dilations = tuple(dil for _, dil in B_GROUPS)
    kern = functools.partial(_inproj_kernel, heads_per_tile=hpt, hq_b=hq_b, hv_b=hv_b, hz_b=hz_b, bh=bh,
                             dilations=dilations)
    perm_spec = pl.BlockSpec((PERM_ROWS, PERM_ROWS), lambda bi, i, j: (0, 0))
    return pl.pallas_call(
        kern,
        grid=(b, s // tm, n // tn),
        in_specs=[
            pl.BlockSpec((1, tm, d), lambda bi, i, j: (bi, i, 0)),
            pl.BlockSpec((d, tn), lambda bi, i, j: (0, j)),
            pl.BlockSpec((tm, LANE), lambda bi, i, j: (i, 0)),
            pl.BlockSpec((tm, LANE), lambda bi, i, j: (i, 0)),
        ] + [perm_spec] * len(perms),
        out_specs=pl.BlockSpec((1, hpt, tm, LANE), lambda bi, i, j: (bi, j, i, 0)),
        out_shape=jax.ShapeDtypeStruct((b, n // LANE, s, LANE), BF16),
        compiler_params=_params("parallel", "parallel", "arbitrary"),
        name="in_proj",
    )(h, w, cos_t, sin_t, *perms)


def _mm_kernel(x_ref, w_ref, o_ref):
    o_ref[0] = _dot(x_ref[0], w_ref[...]).astype(o_ref.dtype)


def _matmul(x, w, out_dtype, name):
    b, s, k = x.shape
    n = w.shape[1]
    tm = _largest_tile(s, (1024, 512, 256, 128))
    tn = _largest_tile(n, (1024, 512, 256, 128))
    return pl.pallas_call(
        _mm_kernel,
        grid=(b, s // tm, n // tn),
        in_specs=[
            pl.BlockSpec((1, tm, k), lambda bi, i, j: (bi, i, 0)),
            pl.BlockSpec((k, tn), lambda bi, i, j: (0, j)),
        ],
        out_specs=pl.BlockSpec((1, tm, tn), lambda bi, i, j: (bi, i, j)),
        out_shape=jax.ShapeDtypeStruct((b, s, n), out_dtype),
        compiler_params=_params("parallel", "parallel", "arbitrary"),
        name=name,
    )(x, w)


def _gate_kernel(ab_ref, alog_ref, dtb_ref, o_ref, *, chunk, heads, tiles):
    C, H = chunk, heads
    row = lax.broadcasted_iota(jnp.int32, (C, C), 0)
    col = lax.broadcasted_iota(jnp.int32, (C, C), 1)
    lane = lax.broadcasted_iota(jnp.int32, (1, LANE), 1)
    lower = (row >= col).astype(F32)
    upper = (row <= col).astype(F32)
    for t in range(tiles):
        rows = slice(t * C, (t + 1) * C)
        ab = ab_ref[0, rows, :]
        gval = -jnp.exp(alog_ref[...]) * jax.nn.softplus(ab + dtb_ref[...])
        prefix = _dot_exact(lower, gval)
        suffix = _dot_exact(upper, gval)
        o_ref[0, rows, :] = jnp.where(lane < H, prefix, jnp.where(lane < 2 * H, suffix, jax.nn.sigmoid(ab)))


def _gates(ab, alog_row, dtb_row, heads):
    b, s, _ = ab.shape
    c = GDN_CHUNK
    tiles = 2 if s % (2 * c) == 0 else 1
    ts = tiles * c
    row_spec = pl.BlockSpec((1, LANE), lambda bi, i: (0, 0))
    tile = pl.BlockSpec((1, ts, LANE), lambda bi, i: (bi, i, 0))
    return pl.pallas_call(
        functools.partial(_gate_kernel, chunk=c, heads=heads, tiles=tiles),
        grid=(b, s // ts),
        in_specs=[tile, row_spec, row_spec],
        out_specs=tile,
        out_shape=jax.ShapeDtypeStruct((b, s, LANE), F32),
        compiler_params=_params("parallel", "parallel"),
        name="deltanet_gates",
    )(ab, alog_row, dtb_row)


def _tri_inverse_stages(lmats, n, result):
    row = lax.broadcasted_iota(jnp.int32, (n, n), 0)
    col = lax.broadcasted_iota(jnp.int32, (n, n), 1)
    eye = jnp.where(row == col, 1.0, 0.0)
    pair = (row >> 1) == (col >> 1)
    ts = [eye - jnp.where(pair, lm, 0.0) for lm in lmats]
    for lv in range(1, int(math.log2(n))):
        join = jnp.logical_and((row >> (lv + 1)) == (col >> (lv + 1)), (row >> lv) != (col >> lv))
        tbs = [t.astype(BF16) for t in ts]
        mids = [_dot(tb, jnp.where(join, lm, 0.0).astype(BF16)) for tb, lm in zip(tbs, lmats)]
        yield
        ts = [t - _dot(mid.astype(BF16), tb) for t, mid, tb in zip(ts, mids, tbs)]
        yield
    result.extend(ts)


def _emit_interleaved(generators):
    active = list(generators)
    while active:
        for gen in list(active):
            try:
                next(gen)
            except StopIteration:
                active.remove(gen)


def _gdn_kernel(q_ref, k_ref, v_ref, z_ref, gate_ref, cq_ref, ck_ref, cv_ref, nw_ref, o_ref,
                oacc, cstage, qn, kn, vn, us, ws, qds, kdt, att, gls, *, seq, chunk, heads):
    S, C, H = seq, chunk, heads
    nc = S // C
    G = GDN_CHUNKS_PER_STEP
    R = C
    nr = S // R
    halo = BF16_SUBLANES
    h = pl.program_id(1)
    lane = lax.broadcasted_iota(jnp.int32, (1, LANE), 1)
    row = lax.broadcasted_iota(jnp.int32, (C, C), 0)
    col = lax.broadcasted_iota(jnp.int32, (C, C), 1)

    def conv_tile(slot, src_ref, cw_ref, dst_ref, norm_scale, r):
        r0 = pl.multiple_of(r * R, R)
        prev0 = pl.multiple_of(jnp.maximum(r0 - halo, 0), halo)
        next0 = pl.multiple_of(jnp.minimum(r0 + R, S - halo), halo)
        prev = src_ref[0, 0, pl.ds(prev0, halo), :].astype(F32)
        nxt = src_ref[0, 0, pl.ds(next0, halo), :].astype(F32)
        cstage[slot, 0:halo, :] = jnp.where(r > 0, prev, 0.0)
        cstage[slot, halo:halo + R, :] = src_ref[0, 0, pl.ds(r0, R), :].astype(F32)
        cstage[slot, halo + R:, :] = jnp.where(r < nr - 1, nxt, 0.0)
        y = jnp.zeros((R, LANE), F32)
        for j in range(CONV_K):
            off = halo - CONV_K // 2 + j
            y = y + cw_ref[j:j + 1, :] * cstage[slot, off:off + R, :]
        y = _silu(y)
        if norm_scale is not None:
            y = y * (lax.rsqrt(jnp.sum(y * y, axis=-1, keepdims=True) + L2_EPS) * norm_scale)
        dst_ref[pl.ds(r0, R), :] = y.astype(dst_ref.dtype)

    def conv_body(i, carry):
        for t in range(2):
            r = 2 * i + t
            conv_tile(3 * t + 0, q_ref, cq_ref, qn, LANE ** -0.5, r)
            conv_tile(3 * t + 1, k_ref, ck_ref, kn, 1.0, r)
            conv_tile(3 * t + 2, v_ref, cv_ref, vn, None, r)
            oacc[pl.ds(pl.multiple_of(r * R, R), R), :] = jnp.zeros((R, LANE), F32)
        return carry

    lax.fori_loop(0, nr // 2, conv_body, 0)

    ng = nc // G
    masks = [((row >= col), (row > col)), ((row <= col), (row < col))]

    def chain_chunks(i):
        return [(0, g, i * G + g) for g in range(G)] + [(1, g, nc - (i + 1) * G + g) for g in range(G)]

    def prepare_stages(i, slot):
        items = []
        for dirn, g, c in chain_chunks(i):
            c0 = pl.multiple_of(c * C, C)
            items.append((dirn, g, qn[pl.ds(c0, C), :], kn[pl.ds(c0, C), :],
                          vn[pl.ds(c0, C), :].astype(F32), gate_ref[0, pl.ds(c0, C), :]))
        kks = [_dot_nt(kb, kb) for (_, _, _, kb, _, _) in items]
        qks = [_dot_nt(qb, kb) for (_, _, qb, kb, _, _) in items]
        yield
        chains = []
        for (dirn, g, qb, kb, vc, gt), kk, qk in zip(items, kks, qks):
            incl, strict = masks[dirn]
            gc = jnp.sum(jnp.where(lane == dirn * H + h, gt, 0.0), axis=1, keepdims=True)
            beta = jnp.sum(jnp.where(lane == (2 + dirn) * H + h, gt, 0.0), axis=1, keepdims=True)
            gcb = jnp.broadcast_to(gc, (C, C))
            decay = jnp.exp(jnp.where(incl, gcb - gcb.T, NEG_INF))
            lmat = jnp.where(strict, (beta * kk) * decay, 0.0)
            amat = jnp.where(incl, qk * decay, 0.0)
            chains.append((dirn, g, qb, kb, vc, gc, beta, amat, lmat))
        tinvs = []
        yield from _tri_inverse_stages([ch[8] for ch in chains], C, tinvs)
        rhss = []
        for (dirn, g, qb, kb, vc, gc, beta, amat, lmat) in chains:
            rhs = jnp.concatenate([vc * beta, kb.astype(F32) * (beta * jnp.exp(gc))], axis=1)
            rhss.append(rhs.astype(BF16))
        uws = [_dot(tinv.astype(BF16), rhs) for tinv, rhs in zip(tinvs, rhss)]
        yield
        for (dirn, g, qb, kb, vc, gc, beta, amat, lmat), uw in zip(chains, uws):
            g_last = gc[C - 1:C, :] if dirn == 0 else gc[0:1, :]
            kd = kb.astype(F32) * jnp.exp(g_last - gc)
            rows = slice(g * C, (g + 1) * C)
            us[slot, dirn, rows, :] = uw[:, :LANE].astype(BF16)
            ws[slot, dirn, rows, :] = uw[:, LANE:].astype(BF16)
            qds[slot, dirn, rows, :] = (qb.astype(F32) * jnp.exp(gc)).astype(BF16)
            kdt[slot, dirn, g * LANE:(g + 1) * LANE, :] = kd.T.astype(BF16)
            att[slot, dirn, rows, :] = amat.astype(BF16)
            gls[slot, dirn, g:g + 1, :] = jnp.broadcast_to(jnp.exp(g_last), (1, LANE))

    def update_stages(i, slot, states, result):
        states = list(states)
        for step in range(G):
            gs = (step, G - 1 - step)
            cs = (i * G + gs[0], nc - (i + 1) * G + gs[1])
            rows = [slice(g * C, (g + 1) * C) for g in gs]
            sbs = [st.astype(BF16) for st in states]
            wss = [_dot(ws[slot, d, rows[d], :], sbs[d]) for d in (0, 1)]
            qss = [_dot(qds[slot, d, rows[d], :], sbs[d]) for d in (0, 1)]
            yield
            vbs = [(us[slot, d, rows[d], :].astype(F32) - wss[d]).astype(BF16) for d in (0, 1)]
            avs = [_dot(att[slot, d, rows[d], :], vbs[d]) for d in (0, 1)]
            kvs = [_dot(kdt[slot, d, gs[d] * LANE:(gs[d] + 1) * LANE, :], vbs[d]) for d in (0, 1)]
            yield
            for d in (0, 1):
                out_rows = pl.ds(pl.multiple_of(cs[d] * C, C), C)
                oacc[out_rows, :] = oacc[out_rows, :] + (qss[d] + avs[d])
            states = [states[d] * gls[slot, d, gs[d]:gs[d] + 1, :] + kvs[d] for d in (0, 1)]
        result.extend(states)

    _emit_interleaved([prepare_stages(0, 0)])

    def pipelined_body(i, states):
        new_states = []
        _emit_interleaved([prepare_stages(i, i % 2), update_stages(i - 1, (i - 1) % 2, states, new_states)])
        return tuple(new_states)

    zero_state = jnp.zeros((LANE, LANE), F32)
    states = lax.fori_loop(1, ng, pipelined_body, (zero_state, zero_state))
    _emit_interleaved([update_stages(ng - 1, (ng - 1) % 2, states, [])])

    def finish_tile(r):
        r0 = pl.multiple_of(r * R, R)
        o = oacc[pl.ds(r0, R), :]
        y = o * lax.rsqrt(jnp.mean(o * o, axis=-1, keepdims=True) + EPS) * nw_ref[...]
        z = z_ref[0, 0, pl.ds(r0, R), :].astype(F32)
        o_ref[0, pl.ds(r0, R), :] = (y * _silu(z)).astype(BF16)

    def finish_body(i, carry):
        finish_tile(2 * i)
        finish_tile(2 * i + 1)
        return carry

    lax.fori_loop(0, nr // 2, finish_body, 0)


def _gdn(p, gates, conv_w, norm_w, heads):
    b, _, s, _ = p.shape
    c = GDN_CHUNK
    nc = s // c
    g = GDN_CHUNKS_PER_STEP
    assert s % (2 * c) == 0 and nc % (2 * g) == 0
    hq, hk, hv, hz = 0, heads, 2 * heads, 3 * heads

    def head_spec(off):
        return pl.BlockSpec((1, 1, s, LANE), lambda bi, hi, off=off: (bi, off + hi, 0, 0))

    def conv_spec(off):
        return pl.BlockSpec((CONV_K, LANE), lambda bi, hi, off=off: (0, off + hi))

    kern = functools.partial(_gdn_kernel, seq=s, chunk=c, heads=heads)
    return pl.pallas_call(
        kern,
        grid=(b, heads),
        in_specs=[
            head_spec(hq), head_spec(hk), head_spec(hv), head_spec(hz),
            _single_buffered((1, s, LANE), lambda bi, hi: (bi, 0, 0)),
            conv_spec(hq), conv_spec(hk), conv_spec(hv),
            pl.BlockSpec((1, LANE), lambda bi, hi: (0, 0)),
        ],
        out_specs=pl.BlockSpec((1, s, LANE), lambda bi, hi: (bi, 0, hi)),
        out_shape=jax.ShapeDtypeStruct((b, s, heads * LANE), BF16),
        scratch_shapes=[
            pltpu.VMEM((s, LANE), F32),
            pltpu.VMEM((6, c + 2 * BF16_SUBLANES, LANE), F32),
            pltpu.VMEM((s, LANE), BF16),
            pltpu.VMEM((s, LANE), BF16),
            pltpu.VMEM((s, LANE), BF16),
            pltpu.VMEM((2, 2, g * c, LANE), BF16),
            pltpu.VMEM((2, 2, g * c, LANE), BF16),
            pltpu.VMEM((2, 2, g * c, LANE), BF16),
            pltpu.VMEM((2, 2, g * LANE, c), BF16),
            pltpu.VMEM((2, 2, g * c, c), BF16),
            pltpu.VMEM((2, 2, max(g, 8), LANE), F32),
        ],
        compiler_params=_params("parallel", "arbitrary"),
        name="gated_deltanet",
    )(p, p, p, p, gates, conv_w, conv_w, conv_w, norm_w)


def _attend_many(blocks, reach):
    scores = [_dot_nt(q, k) * (LANE ** -0.5) for (q, k, _, _, _) in blocks]
    probs, stats = [], []
    for s, (q, k, _, q0, k0) in zip(scores, blocks):
        tq, w = q.shape[0], k.shape[0]
        qpos = q0 + lax.broadcasted_iota(jnp.int32, (tq, w), 0)
        kpos = k0 + lax.broadcasted_iota(jnp.int32, (tq, w), 1)
        s = jnp.where(jnp.abs(kpos - qpos) <= reach, s, NEG_INF)
        m = jnp.max(s, axis=1, keepdims=True)
        p = jnp.exp(s - m)
        den = jnp.sum(p, axis=1, keepdims=True)
        probs.append(p.astype(BF16))
        stats.append((m, den))
    outs = [_dot(p, v) for p, (_, _, v, _, _) in zip(probs, blocks)]
    return [((o / den).astype(BF16), jnp.broadcast_to(m + jnp.log(den), (o.shape[0], LANE)))
            for o, (m, den) in zip(outs, stats)]


def _attn_kernel(q_ref, k_ref, v_ref, o_ref, lse_ref, *, length, reach, rows, dilation, unroll, residues):
    L, R, TQ, U = length, reach, ATTN_TQ, unroll
    W = TQ + 2 * R

    def residue_group(rg, carry):
        def body(step, c):
            blocks, where = [], []
            for t in range(residues):
                r = rg * residues + t
                for u in range(U):
                    i = step * U + u
                    q0 = pl.multiple_of(i * TQ, TQ)
                    k0 = pl.multiple_of(jnp.clip(q0 - R, 0, L - W), R)
                    if rows:
                        qg = pl.ds(pl.multiple_of(q0 // rows, TQ // rows), TQ // rows)
                        kg = pl.ds(pl.multiple_of(k0 // rows, R // rows), W // rows)
                        blocks.append((q_ref[0, 0, qg, r].reshape(TQ, LANE),
                                       k_ref[0, 0, kg, r].reshape(W, LANE),
                                       v_ref[0, 0, kg, r].reshape(W, LANE), q0, k0))
                    else:
                        blocks.append((q_ref[0, 0, pl.ds(q0, TQ), :], k_ref[0, 0, pl.ds(k0, W), :],
                                       v_ref[0, 0, pl.ds(k0, W), :], q0, k0))
                    where.append((r, i))
            for (r, i), (o, lse) in zip(where, _attend_many(blocks, R)):
                if rows:
                    for j in range(TQ // rows):
                        part = slice(j * rows, (j + 1) * rows)
                        o_ref[0, 0, i * (TQ // rows) + j, r] = o[part, :]
                        lse_ref[0, 0, i * (TQ // rows) + j, r] = lse[part, :]
                else:
                    q0 = pl.multiple_of(i * TQ, TQ)
                    o_ref[0, 0, pl.ds(q0, TQ), :] = o
                    lse_ref[0, 0, pl.ds(q0, TQ), :] = lse
            return c

        lax.fori_loop(0, L // TQ // U, body, 0)
        return carry

    if rows:
        lax.fori_loop(0, dilation // residues, residue_group, 0)
    else:
        residue_group(0, 0)


def _window_attention(p, hq, hk, hv, heads, window, dilation):
    b, nh, s, _ = p.shape
    reach = window // (2 * dilation)
    length = s // dilation
    unroll = math.gcd(ATTN_BLOCKS_PER_STEP, length // ATTN_TQ)
    residues = math.gcd(ATTN_BLOCKS_PER_STEP // unroll, dilation)
    assert length % ATTN_TQ == 0 and length >= ATTN_TQ + 2 * reach
    if dilation == 1:
        rows = 0
        pv = p
        out_dims = (b, heads, s, LANE)
        block = (1, 1, s, LANE)
        tail = (0, 0)
    else:
        rows = PERM_ROWS // dilation
        assert rows % BF16_SUBLANES == 0 and ATTN_TQ % rows == 0 and reach % rows == 0 and s % PERM_ROWS == 0
        ng = s // PERM_ROWS
        pv = p.reshape(b, nh, ng, dilation, rows, LANE)
        out_dims = (b, heads, ng, dilation, rows, LANE)
        block = (1, 1, ng, dilation, rows, LANE)
        tail = (0, 0, 0, 0)

    def spec(off):
        return pl.BlockSpec(block, lambda bi, hi, off=off: (bi, off + hi) + tail)

    kern = functools.partial(_attn_kernel, length=length, reach=reach, rows=rows, dilation=dilation,
                             unroll=unroll, residues=residues)
    o, lse = pl.pallas_call(
        kern,
        grid=(b, heads),
        in_specs=[spec(hq), spec(hk), spec(hv)],
        out_specs=[spec(0), spec(0)],
        out_shape=[jax.ShapeDtypeStruct(out_dims, BF16), jax.ShapeDtypeStruct(out_dims, F32)],
        compiler_params=_params("parallel", "parallel"),
        name=f"window_attention_d{dilation}",
    )(pv, pv, pv)
    return o.reshape(b, heads, s, LANE), lse.reshape(b, heads, s, LANE)


def _select_rows_packed(sel, xs):
    assert 3 * len(xs) <= LANE
    lane = lax.broadcasted_iota(jnp.int32, (1, LANE), 1)
    packed = jnp.zeros(xs[0].shape, F32)
    for a, x in enumerate(xs):
        hi = x.astype(BF16).astype(F32)
        rest = x - hi
        mid = rest.astype(BF16).astype(F32)
        lo = (rest - mid).astype(BF16).astype(F32)
        for t, term in enumerate((hi, mid, lo)):
            packed = jnp.where(lane == 3 * a + t, term, packed)
    moved = _dot(sel, packed.astype(BF16))
    outs = []
    for a in range(len(xs)):
        mine = jnp.logical_and(lane >= 3 * a, lane < 3 * a + 3)
        col = jnp.sum(jnp.where(mine, moved, 0.0), axis=1, keepdims=True)
        outs.append(jnp.broadcast_to(col, moved.shape))
    return outs


def _combine_kernel(*refs, heads, dilations):
    ng = len(dilations)
    o_refs, l_refs = refs[:ng], refs[ng:2 * ng]
    z_ref = refs[2 * ng]
    unperm_refs = refs[2 * ng + 1:-1]
    y_ref = refs[-1]
    unperm_of = {}
    for d in dilations:
        if d > 1:
            unperm_of[d] = unperm_refs[len(unperm_of)]
    lse_of = {}
    for g, d in enumerate(dilations):
        if d > 1:
            lse_of[g] = _select_rows_packed(unperm_of[d][...], [l_refs[g][0, h] for h in range(heads)])
    for h in range(heads):
        outs, lses = [], []
        for g, d in enumerate(dilations):
            if d > 1:
                outs.append(_dot(unperm_of[d][...], o_refs[g][0, h]))
                lses.append(lse_of[g][h])
            else:
                outs.append(o_refs[g][0, h].astype(F32))
                lses.append(l_refs[g][0, h])
        m = functools.reduce(jnp.maximum, lses)
        wts = [jnp.exp(l - m) for l in lses]
        den = functools.reduce(lambda a, c: a + c, wts)
        acc = functools.reduce(lambda a, c: a + c, [w * o for w, o in zip(wts, outs)])
        z = z_ref[0, h].astype(F32)
        y_ref[0, :, h * LANE:(h + 1) * LANE] = (acc / den * _silu(z)).astype(BF16)


def _combine(outs, lses, unperms, p, hz, heads):
    b, _, s, _ = p.shape
    ts = PERM_ROWS
    assert hz % heads == 0 and s % ts == 0
    dilations = tuple(dil for _, dil in B_GROUPS)
    spec = pl.BlockSpec((1, heads, ts, LANE), lambda bi, i: (bi, 0, i, 0))
    n = len(outs)
    return pl.pallas_call(
        functools.partial(_combine_kernel, heads=heads, dilations=dilations),
        grid=(b, s // ts),
        in_specs=[spec] * (2 * n)
        + [pl.BlockSpec((1, heads, ts, LANE), lambda bi, i: (bi, hz // heads, i, 0))]
        + [pl.BlockSpec((PERM_ROWS, PERM_ROWS), lambda bi, i: (0, 0))] * len(unperms),
        out_specs=pl.BlockSpec((1, ts, heads * LANE), lambda bi, i: (bi, i, 0)),
        out_shape=jax.ShapeDtypeStruct((b, s, heads * LANE), BF16),
        compiler_params=_params("parallel", "parallel"),
        name="group_combine",
    )(*outs, *lses, p, *unperms)


def _merge_kernel(ya_ref, yb_ref, wa_ref, wb_ref, ga_ref, gb_ref, o_ref, *, heads_per_tile):
    acc_a = _dot(ya_ref[0], wa_ref[...])
    acc_b = _dot(yb_ref[0], wb_ref[...])
    for k in range(heads_per_tile):
        sl = slice(k * LANE, (k + 1) * LANE)
        ga = jax.nn.sigmoid(ga_ref[0, k].astype(F32))
        gb = jax.nn.sigmoid(gb_ref[0, k].astype(F32))
        o_ref[0, :, sl] = (ga * acc_a[:, sl] + gb * acc_b[:, sl]).astype(BF16)


def _merge(ya, yb, wa, wb, p, hga, hgb):
    b, s, ka = ya.shape
    kb = yb.shape[2]
    d = wa.shape[1]
    tm = _largest_tile(s, (1024, 512, 256, 128))
    tn = _largest_tile(math.gcd(d, hga * LANE, hgb * LANE), (1024, 512, 256, 128))
    hpt = tn // LANE
    return pl.pallas_call(
        functools.partial(_merge_kernel, heads_per_tile=hpt),
        grid=(b, s // tm, d // tn),
        in_specs=[
            pl.BlockSpec((1, tm, ka), lambda bi, i, j: (bi, i, 0)),
            pl.BlockSpec((1, tm, kb), lambda bi, i, j: (bi, i, 0)),
            pl.BlockSpec((ka, tn), lambda bi, i, j: (0, j)),
            pl.BlockSpec((kb, tn), lambda bi, i, j: (0, j)),
            pl.BlockSpec((1, hpt, tm, LANE), lambda bi, i, j: (bi, hga // hpt + j, i, 0)),
            pl.BlockSpec((1, hpt, tm, LANE), lambda bi, i, j: (bi, hgb // hpt + j, i, 0)),
        ],
        out_specs=pl.BlockSpec((1, tm, tn), lambda bi, i, j: (bi, i, j)),
        out_shape=jax.ShapeDtypeStruct((b, s, d), BF16),
        compiler_params=_params("parallel", "parallel", "arbitrary"),
        name="branch_merge",
    )(ya, yb, wa, wb, p, p)


def _final_kernel(*refs, with_next):
    if with_next:
        x_ref, out_ref, gate_ref, gpost_ref, sh_ref, sc_ref, gpre_ref, y_ref, h_ref = refs
    else:
        x_ref, out_ref, gate_ref, gpost_ref, y_ref = refs
    out = out_ref[0]
    normed = out * lax.rsqrt(jnp.mean(out * out, axis=-1, keepdims=True) + EPS) * gpost_ref[...]
    y = x_ref[0] + gate_ref[0] * normed
    y_ref[0] = y
    if with_next:
        h_ref[0] = _modulated_norm(y, gpre_ref[...], sc_ref[0], sh_ref[0]).astype(BF16)


def _finalize(x, out, mod, g_post, next_mod, next_gain):
    b, s, d = x.shape
    ts = _largest_tile(s, (256, 128, 64, 32, 16, 8))
    with_next = next_mod is not None
    tile = pl.BlockSpec((1, ts, d), lambda bi, i: (bi, i, 0))
    row = pl.BlockSpec((1, d), lambda bi, i: (0, 0))

    def mod_spec(part):
        return pl.BlockSpec((1, 1, d), lambda bi, i, part=part: (bi, 0, part))

    in_specs = [tile, tile, mod_spec(2), row]
    args = [x, out, mod, g_post.reshape(1, d)]
    out_specs = [tile]
    out_shape = [jax.ShapeDtypeStruct((b, s, d), F32)]
    if with_next:
        in_specs += [mod_spec(0), mod_spec(1), row]
        args += [next_mod, next_mod, next_gain.reshape(1, d)]
        out_specs.append(tile)
        out_shape.append(jax.ShapeDtypeStruct((b, s, d), BF16))
    res = pl.pallas_call(
        functools.partial(_final_kernel, with_next=with_next),
        grid=(b, s // ts),
        in_specs=in_specs,
        out_specs=out_specs,
        out_shape=out_shape,
        compiler_params=_params("parallel", "parallel"),
        name="postnorm_residual",
    )(*args)
    return (res[0], res[1]) if with_next else (res[0], None)


def _rope_tables(s):
    half = LANE // 2
    inv = ROPE_THETA ** (-jnp.arange(half, dtype=F32) / half)
    ang = jnp.arange(s, dtype=F32)[:, None] * inv[None, :]
    cos, sin = jnp.cos(ang), jnp.sin(ang)
    return jnp.concatenate([cos, cos], axis=1), jnp.concatenate([-sin, sin], axis=1)


def _pad_row(v):
    v = v.reshape(1, -1).astype(F32)
    return jnp.pad(v, ((0, 0), (0, LANE - v.shape[1])))


def kernel(x_prompt, x_sample, c_prompt, c_sample, w_ada, b_ada, norm_pre, norm_post, w_in, conv_a,
           a_log, dt_bias, norm_a, w_up_a, w_up_b, w_out):
    depth, d, _ = w_in.shape
    ah = d // 256
    bh = d // 512
    aw = ah * LANE
    n_ab = 4 * ah
    assert n_ab <= LANE
    hq_b = 4 * ah
    hk_b = hq_b + 3 * bh
    hv_b = hk_b + 3 * bh
    hz_b = hv_b + 3 * bh
    hg_a = hz_b + bh
    hg_b = hg_a + d // LANE
    c0 = 4 * aw
    c1 = c0 + n_ab

    bp, bs = c_prompt.shape[0], c_sample.shape[0]
    c_all = jnp.concatenate([c_prompt, c_sample], axis=0)
    c_all = jnp.pad(c_all, ((0, -c_all.shape[0] % 8), (0, 0)))
    mod_all = _modulation(c_all, w_ada, b_ada)

    w_main = [jnp.concatenate([w_in[l, :, :c0], w_in[l, :, c1:]], axis=1).astype(BF16) for l in range(depth)]
    w_ab = [jnp.pad(w_in[l, :, c0:c1], ((0, 0), (0, LANE - n_ab))).astype(BF16) for l in range(depth)]
    wa = [w_up_a[l].astype(BF16) for l in range(depth)]
    wb = [w_up_b[l].astype(BF16) for l in range(depth)]
    wo = [w_out[l].astype(BF16) for l in range(depth)]
    perms = [_residue_permutation(dil) for _, dil in B_GROUPS if dil > 1]
    unperms = [pm.T for pm in perms]

    def trunk(x, row0, nb):
        s = x.shape[1]
        cos_t, sin_t = _rope_tables(s)
        mods = [mod_all[l, row0:row0 + nb].reshape(nb, 1, 3 * d) for l in range(depth)]
        h = _prenorm(x, mods[0], norm_pre[0])
        for l in range(depth):
            p = _inproj(h, w_main[l], cos_t, sin_t, perms, hq_b, hv_b, hz_b, bh)
            ab = _matmul(h, w_ab[l], F32, "in_proj_gates")
            gates = _gates(ab, _pad_row(a_log[l]), _pad_row(dt_bias[l]), ah)
            ya = _gdn(p, gates, conv_a[l], norm_a[l].reshape(1, LANE), ah)
            outs, lses = [], []
            for gi, (window, dilation) in enumerate(B_GROUPS):
                o, lse = _window_attention(p, hq_b + gi * bh, hk_b + gi * bh, hv_b + gi * bh, bh,
                                           window, dilation)
                outs.append(o)
                lses.append(lse)
            yb = _combine(outs, lses, unperms, p, hz_b, bh)
            merged = _merge(ya, yb, wa[l], wb[l], p, hg_a, hg_b)
            out = _matmul(merged, wo[l], F32, "out_proj")
            last = l == depth - 1
            x, h = _finalize(x, out, mods[l], norm_post[l],
                             None if last else mods[l + 1], None if last else norm_pre[l + 1])
        return x

    y_prompt = trunk(x_prompt, 0, bp)
    y_sample = trunk(x_sample, bp, bs)
    return (y_prompt, y_sample)
```

```python
import functools
import math

import jax
import jax.numpy as jnp
from jax import lax
from jax.experimental import pallas as pl
from jax.experimental.pallas import tpu as pltpu

F32 = jnp.float32
BF16 = jnp.bfloat16

LANE = 128
BF16_SUBLANES = 16
V7X_VMEM_BYTES = 64 * 1024 * 1024
VMEM_LIMIT_BYTES = V7X_VMEM_BYTES - 8 * 1024 * 1024

EPS = 1e-6
L2_EPS = 1e-6
ROPE_THETA = 10000.0
NEG_INF = -1e30
CONV_K = 5
B_GROUPS = ((128, 1), (512, 4), (2048, 16))
GDN_CHUNK = 128
GDN_CHUNKS_PER_STEP = 8
ATTN_TQ = 128
ATTN_BLOCKS_PER_STEP = 8
PERM_ROWS = 256


def _params(*sem):
    return pltpu.CompilerParams(dimension_semantics=sem, vmem_limit_bytes=VMEM_LIMIT_BYTES)


def _largest_tile(n, candidates):
    for c in candidates:
        if n % c == 0:
            return c
    raise ValueError(f"no tile in {candidates} divides {n}")


def _dot(a, b):
    return jnp.dot(a, b, preferred_element_type=F32)


def _dot_exact(a, b):
    return jnp.dot(a, b, precision=lax.Precision.HIGHEST, preferred_element_type=F32)


def _dot_nt(a, b):
    return lax.dot_general(a, b, (((1,), (1,)), ((), ())), preferred_element_type=F32)


def _silu(x):
    return x * jax.nn.sigmoid(x)


def _single_buffered(shape, index_map):
    return pl.BlockSpec(shape, index_map, pipeline_mode=pl.Buffered(1))


def _mod_kernel(c_ref, w_ref, b_ref, o_ref):
    c = c_ref[...]
    o_ref[0] = _dot(_silu(c).astype(BF16), w_ref[0].astype(BF16)) + b_ref[0]


def _modulation(c_all, w_ada, b_ada):
    depth, d, n = w_ada.shape
    r = c_all.shape[0]
    tn = _largest_tile(n, (512, 256, 128))
    return pl.pallas_call(
        _mod_kernel,
        grid=(depth, n // tn),
        in_specs=[
            pl.BlockSpec((r, d), lambda l, j: (0, 0)),
            pl.BlockSpec((1, d, tn), lambda l, j: (l, 0, j)),
            pl.BlockSpec((1, 1, tn), lambda l, j: (l, 0, j)),
        ],
        out_specs=pl.BlockSpec((1, r, tn), lambda l, j: (l, 0, j)),
        out_shape=jax.ShapeDtypeStruct((depth, r, n), F32),
        compiler_params=_params("parallel", "parallel"),
        name="adaln_mod",
    )(c_all, w_ada, b_ada.reshape(depth, 1, n))


def _modulated_norm(x, gain, scale, shift):
    y = x * lax.rsqrt(jnp.mean(x * x, axis=-1, keepdims=True) + EPS)
    return (y * gain) * (1.0 + scale) + shift


def _prenorm_kernel(x_ref, sh_ref, sc_ref, g_ref, o_ref):
    o_ref[0] = _modulated_norm(x_ref[0], g_ref[...], sc_ref[0], sh_ref[0]).astype(BF16)


def _prenorm(x, mod, gain):
    b, s, d = x.shape
    ts = _largest_tile(s, (256, 128, 64, 32, 16, 8))
    return pl.pallas_call(
        _prenorm_kernel,
        grid=(b, s // ts),
        in_specs=[
            pl.BlockSpec((1, ts, d), lambda bi, i: (bi, i, 0)),
            pl.BlockSpec((1, 1, d), lambda bi, i: (bi, 0, 0)),
            pl.BlockSpec((1, 1, d), lambda bi, i: (bi, 0, 1)),
            pl.BlockSpec((1, d), lambda bi, i: (0, 0)),
        ],
        out_specs=pl.BlockSpec((1, ts, d), lambda bi, i: (bi, i, 0)),
        out_shape=jax.ShapeDtypeStruct((b, s, d), BF16),
        compiler_params=_params("parallel", "parallel"),
        name="prenorm",
    )(x, mod, mod, gain.reshape(1, d))


def _residue_permutation(dilation):
    i = jnp.arange(PERM_ROWS)
    src = (i % (PERM_ROWS // dilation)) * dilation + i // (PERM_ROWS // dilation)
    return (src[:, None] == jnp.arange(PERM_ROWS)[None, :]).astype(BF16)


def _inproj_kernel(h_ref, w_ref, cos_ref, sin_ref, *rest, heads_per_tile, hq_b, hv_b, hz_b, bh, dilations):
    perm_refs, o_ref = rest[:-1], rest[-1]
    tm = h_ref.shape[1]
    acc = _dot(h_ref[0], w_ref[...])
    perm_of = {}
    for d in dilations:
        if d > 1:
            perm_of[d] = perm_refs[len(perm_of)]
    head0 = pl.program_id(0) * heads_per_tile
    in_b = jnp.logical_and(head0 >= hq_b, head0 < hz_b)
    group = jnp.where(in_b, ((head0 - hq_b) // bh) % len(dilations), 0)
    is_rope = jnp.logical_and(head0 >= hq_b, head0 < hv_b)
    for k in range(heads_per_tile):
        o_ref[0, k] = acc[:, k * LANE:(k + 1) * LANE].astype(BF16)
    for rope in (False, True):
        rope_matches = is_rope if rope else jnp.logical_not(is_rope)
        for g, d in enumerate(dilations):
            if not rope and d == 1:
                continue

            @pl.when(jnp.logical_and(rope_matches, group == g))
            def _(rope=rope, d=d):
                slabs = []
                for k in range(heads_per_tile):
                    slab = acc[:, k * LANE:(k + 1) * LANE]
                    if rope:
                        slab = slab * cos_ref[...] + pltpu.roll(slab, LANE // 2, 1) * sin_ref[...]
                    slabs.append(slab.astype(BF16))
                if d == 1:
                    for k in range(heads_per_tile):
                        o_ref[0, k] = slabs[k]
                else:
                    tile = jnp.concatenate(slabs, axis=1)
                    for s in range(tm // PERM_ROWS):
                        rows = slice(s * PERM_ROWS, (s + 1) * PERM_ROWS)
                        moved = _dot(perm_of[d][...], tile[rows, :]).astype(BF16)
                        for k in range(heads_per_tile):
                            o_ref[0, k, rows, :] = moved[:, k * LANE:(k + 1) * LANE]


def _inproj(h, w, cos_t, sin_t, perms, hq_b, hv_b, hz_b, bh):
    b, s, d = h.shape
    n = w.shape[1]
    tm = _largest_tile(s, (1024, 512, 256))
    tn = _largest_tile(math.gcd(n, bh * LANE), (1024, 512, 256, 128))
    hpt = tn // LANE
    assert hq_b % hpt == 0
    dilations = tuple(dil for _, dil in B_GROUPS)
    kern = functools.partial(_inproj_kernel, heads_per_tile=hpt, hq_b=hq_b, hv_b=hv_b, hz_b=hz_b, bh=bh,
                             dilations=dilations)
    perm_spec = pl.BlockSpec((PERM_ROWS, PERM_ROWS), lambda j, bi, i: (0, 0))
    return pl.pallas_call(
        kern,
        grid=(n // tn, b, s // tm),
        in_specs=[
            pl.BlockSpec((1, tm, d), lambda j, bi, i: (bi, i, 0)),
            pl.BlockSpec((d, tn), lambda j, bi, i: (0, j)),
            pl.BlockSpec((tm, LANE), lambda j, bi, i: (i, 0)),
            pl.BlockSpec((tm, LANE), lambda j, bi, i: (i, 0)),
        ] + [perm_spec] * len(perms),
        out_specs=pl.BlockSpec((1, hpt, tm, LANE), lambda j, bi, i: (bi, j, i, 0)),
        out_shape=jax.ShapeDtypeStruct((b, n // LANE, s, LANE), BF16),
        compiler_params=_params("arbitrary", "arbitrary", "arbitrary"),
        name="in_proj",
    )(h, w, cos_t, sin_t, *perms)


def _mm_kernel(x_ref, w_ref, o_ref):
    o_ref[0] = _dot(x_ref[0], w_ref[...]).astype(o_ref.dtype)


def _matmul(x, w, out_dtype, name):
    b, s, k = x.shape
    n = w.shape[1]
    tm = _largest_tile(s, (1024, 512, 256, 128))
    tn = _largest_tile(n, (1024, 512, 256, 128))
    return pl.pallas_call(
        _mm_kernel,
        grid=(b, s // tm, n // tn),
        in_specs=[
            pl.BlockSpec((1, tm, k), lambda bi, i, j: (bi, i, 0)),
            pl.BlockSpec((k, tn), lambda bi, i, j: (0, j)),
        ],
        out_specs=pl.BlockSpec((1, tm, tn), lambda bi, i, j: (bi, i, j)),
        out_shape=jax.ShapeDtypeStruct((b, s, n), out_dtype),
        compiler_params=_params("parallel", "parallel", "arbitrary"),
        name=name,
    )(x, w)


def _gate_kernel(ab_ref, alog_ref, dtb_ref, o_ref, *, chunk, heads, tiles):
    C, H = chunk, heads
    row = lax.broadcasted_iota(jnp.int32, (C, C), 0)
    col = lax.broadcasted_iota(jnp.int32, (C, C), 1)
    lane = lax.broadcasted_iota(jnp.int32, (1, LANE), 1)
    lower = (row >= col).astype(F32)
    upper = (row <= col).astype(F32)
    for t in range(tiles):
        rows = slice(t * C, (t + 1) * C)
        ab = ab_ref[0, rows, :]
        gval = -jnp.exp(alog_ref[...]) * jax.nn.softplus(ab + dtb_ref[...])
        prefix = _dot_exact(lower, gval)
        suffix = _dot_exact(upper, gval)
        o_ref[0, rows, :] = jnp.where(lane < H, prefix, jnp.where(lane < 2 * H, suffix, jax.nn.sigmoid(ab)))


def _gates(ab, alog_row, dtb_row, heads):
    b, s, _ = ab.shape
    c = GDN_CHUNK
    tiles = 2 if s % (2 * c) == 0 else 1
    ts = tiles * c
    row_spec = pl.BlockSpec((1, LANE), lambda bi, i: (0, 0))
    tile = pl.BlockSpec((1, ts, LANE), lambda bi, i: (bi, i, 0))
    return pl.pallas_call(
        functools.partial(_gate_kernel, chunk=c, heads=heads, tiles=tiles),
        grid=(b, s // ts),
        in_specs=[tile, row_spec, row_spec],
        out_specs=tile,
        out_shape=jax.ShapeDtypeStruct((b, s, LANE), F32),
        compiler_params=_params("parallel", "parallel"),
        name="deltanet_gates",
    )(ab, alog_row, dtb_row)


def _tri_inverse_stages(lmats, n, result):
    row = lax.broadcasted_iota(jnp.int32, (n, n), 0)
    col = lax.broadcasted_iota(jnp.int32, (n, n), 1)
    eye = jnp.where(row == col, 1.0, 0.0)
    pair = (row >> 1) == (col >> 1)
    ts = [eye - jnp.where(pair, lm, 0.0) for lm in lmats]
    for lv in range(1, int(math.log2(n))):
        join = jnp.logical_and((row >> (lv + 1)) == (col >> (lv + 1)), (row >> lv) != (col >> lv))
        tbs = [t.astype(BF16) for t in ts]
        mids = [_dot(tb, jnp.where(join, lm, 0.0).astype(BF16)) for tb, lm in zip(tbs, lmats)]
        yield
        ts = [t - _dot(mid.astype(BF16), tb) for t, mid, tb in zip(ts, mids, tbs)]
        yield
    result.extend(ts)


def _emit_interleaved(generators):
    active = list(generators)
    while active:
        for gen in list(active):
            try:
                next(gen)
            except StopIteration:
                active.remove(gen)


def _gdn_kernel(q_ref, k_ref, v_ref, z_ref, gate_ref, cq_ref, ck_ref, cv_ref, nw_ref, o_ref,
                oacc, cstage, qn, kn, vn, us, ws, qds, kdt, att, gls, *, seq, chunk, heads):
    S, C, H = seq, chunk, heads
    nc = S // C
    G = GDN_CHUNKS_PER_STEP
    R = C
    nr = S // R
    halo = BF16_SUBLANES
    h = pl.program_id(1)
    lane = lax.broadcasted_iota(jnp.int32, (1, LANE), 1)
    row = lax.broadcasted_iota(jnp.int32, (C, C), 0)
    col = lax.broadcasted_iota(jnp.int32, (C, C), 1)

    def conv_tile(slot, src_ref, cw_ref, dst_ref, norm_scale, r):
        r0 = pl.multiple_of(r * R, R)
        prev0 = pl.multiple_of(jnp.maximum(r0 - halo, 0), halo)
        next0 = pl.multiple_of(jnp.minimum(r0 + R, S - halo), halo)
        prev = src_ref[0, 0, pl.ds(prev0, halo), :].astype(F32)
        nxt = src_ref[0, 0, pl.ds(next0, halo), :].astype(F32)
        cstage[slot, 0:halo, :] = jnp.where(r > 0, prev, 0.0)
        cstage[slot, halo:halo + R, :] = src_ref[0, 0, pl.ds(r0, R), :].astype(F32)
        cstage[slot, halo + R:, :] = jnp.where(r < nr - 1, nxt, 0.0)
        y = jnp.zeros((R, LANE), F32)
        for j in range(CONV_K):
            off = halo - CONV_K // 2 + j
            y = y + cw_ref[j:j + 1, :] * cstage[slot, off:off + R, :]
        y = _silu(y)
        if norm_scale is not None:
            y = y * (lax.rsqrt(jnp.sum(y * y, axis=-1, keepdims=True) + L2_EPS) * norm_scale)
        dst_ref[pl.ds(r0, R), :] = y.astype(dst_ref.dtype)

    def conv_body(i, carry):
        for t in range(2):
            r = 2 * i + t
            conv_tile(3 * t + 0, q_ref, cq_ref, qn, LANE ** -0.5, r)
            conv_tile(3 * t + 1, k_ref, ck_ref, kn, 1.0, r)
            conv_tile(3 * t + 2, v_ref, cv_ref, vn, None, r)
            oacc[pl.ds(pl.multiple_of(r * R, R), R), :] = jnp.zeros((R, LANE), F32)
        return carry

    lax.fori_loop(0, nr // 2, conv_body, 0)

    ng = nc // G
    masks = [((row >= col), (row > col)), ((row <= col), (row < col))]

    def chain_chunks(i):
        return [(0, g, i * G + g) for g in range(G)] + [(1, g, nc - (i + 1) * G + g) for g in range(G)]

    def prepare_stages(i, slot):
        items = []
        for dirn, g, c in chain_chunks(i):
            c0 = pl.multiple_of(c * C, C)
            items.append((dirn, g, qn[pl.ds(c0, C), :], kn[pl.ds(c0, C), :],
                          vn[pl.ds(c0, C), :].astype(F32), gate_ref[0, pl.ds(c0, C), :]))
        kks = [_dot_nt(kb, kb) for (_, _, _, kb, _, _) in items]
        qks = [_dot_nt(qb, kb) for (_, _, qb, kb, _, _) in items]
        yield
        chains = []
        for (dirn, g, qb, kb, vc, gt), kk, qk in zip(items, kks, qks):
            incl, strict = masks[dirn]
            gc = jnp.sum(jnp.where(lane == dirn * H + h, gt, 0.0), axis=1, keepdims=True)
            beta = jnp.sum(jnp.where(lane == (2 + dirn) * H + h, gt, 0.0), axis=1, keepdims=True)
            gcb = jnp.broadcast_to(gc, (C, C))
            decay = jnp.exp(jnp.where(incl, gcb - gcb.T, NEG_INF))
            lmat = jnp.where(strict, (beta * kk) * decay, 0.0)
            amat = jnp.where(incl, qk * decay, 0.0)
            chains.append((dirn, g, qb, kb, vc, gc, beta, amat, lmat))
        tinvs = []
        yield from _tri_inverse_stages([ch[8] for ch in chains], C, tinvs)
        rhss = []
        for (dirn, g, qb, kb, vc, gc, beta, amat, lmat) in chains:
            rhs = jnp.concatenate([vc * beta, kb.astype(F32) * (beta * jnp.exp(gc))], axis=1)
            rhss.append(rhs.astype(BF16))
        uws = [_dot(tinv.astype(BF16), rhs) for tinv, rhs in zip(tinvs, rhss)]
        yield
        for (dirn, g, qb, kb, vc, gc, beta, amat, lmat), uw in zip(chains, uws):
            g_last = gc[C - 1:C, :] if dirn == 0 else gc[0:1, :]
            kd = kb.astype(F32) * jnp.exp(g_last - gc)
            rows = slice(g * C, (g + 1) * C)
            us[slot, dirn, rows, :] = uw[:, :LANE].astype(BF16)
            ws[slot, dirn, rows, :] = uw[:, LANE:].astype(BF16)
            qds[slot, dirn, rows, :] = (qb.astype(F32) * jnp.exp(gc)).astype(BF16)
            kdt[slot, dirn, g * LANE:(g + 1) * LANE, :] = kd.T.astype(BF16)
            att[slot, dirn, rows, :] = amat.astype(BF16)
            gls[slot, dirn, g:g + 1, :] = jnp.broadcast_to(jnp.exp(g_last), (1, LANE))

    def update_stages(i, slot, states, result):
        states = list(states)
        for step in range(G):
            gs = (step, G - 1 - step)
            cs = (i * G + gs[0], nc - (i + 1) * G + gs[1])
            rows = [slice(g * C, (g + 1) * C) for g in gs]
            sbs = [st.astype(BF16) for st in states]
            wss = [_dot(ws[slot, d, rows[d], :], sbs[d]) for d in (0, 1)]
            qss = [_dot(qds[slot, d, rows[d], :], sbs[d]) for d in (0, 1)]
            yield
            vbs = [(us[slot, d, rows[d], :].astype(F32) - wss[d]).astype(BF16) for d in (0, 1)]
            avs = [_dot(att[slot, d, rows[d], :], vbs[d]) for d in (0, 1)]
            kvs = [_dot(kdt[slot, d, gs[d] * LANE:(gs[d] + 1) * LANE, :], vbs[d]) for d in (0, 1)]
            yield
            for d in (0, 1):
                out_rows = pl.ds(pl.multiple_of(cs[d] * C, C), C)
                oacc[out_rows, :] = oacc[out_rows, :] + (qss[d] + avs[d])
            states = [states[d] * gls[slot, d, gs[d]:gs[d] + 1, :] + kvs[d] for d in (0, 1)]
        result.extend(states)

    _emit_interleaved([prepare_stages(0, 0)])

    def pipelined_body(i, states):
        new_states = []
        _emit_interleaved([prepare_stages(i, i % 2), update_stages(i - 1, (i - 1) % 2, states, new_states)])
        return tuple(new_states)

    zero_state = jnp.zeros((LANE, LANE), F32)
    states = lax.fori_loop(1, ng, pipelined_body, (zero_state, zero_state))
    _emit_interleaved([update_stages(ng - 1, (ng - 1) % 2, states, [])])

    def finish_tile(r):
        r0 = pl.multiple_of(r * R, R)
        o = oacc[pl.ds(r0, R), :]
        y = o * lax.rsqrt(jnp.mean(o * o, axis=-1, keepdims=True) + EPS) * nw_ref[...]
        z = z_ref[0, 0, pl.ds(r0, R), :].astype(F32)
        o_ref[0, pl.ds(r0, R), :] = (y * _silu(z)).astype(BF16)

    def finish_body(i, carry):
        finish_tile(2 * i)
        finish_tile(2 * i + 1)
        return carry

    lax.fori_loop(0, nr // 2, finish_body, 0)


def _gdn(p, gates, conv_w, norm_w, heads):
    b, _, s, _ = p.shape
    c = GDN_CHUNK
    nc = s // c
    g = GDN_CHUNKS_PER_STEP
    assert s % (2 * c) == 0 and nc % (2 * g) == 0
    hq, hk, hv, hz = 0, heads, 2 * heads, 3 * heads

    def head_spec(off):
        return pl.BlockSpec((1, 1, s, LANE), lambda bi, hi, off=off: (bi, off + hi, 0, 0))

    def conv_spec(off):
        return pl.BlockSpec((CONV_K, LANE), lambda bi, hi, off=off: (0, off + hi))

    kern = functools.partial(_gdn_kernel, seq=s, chunk=c, heads=heads)
    return pl.pallas_call(
        kern,
        grid=(b, heads),
        in_specs=[
            head_spec(hq), head_spec(hk), head_spec(hv), head_spec(hz),
            _single_buffered((1, s, LANE), lambda bi, hi: (bi, 0, 0)),
            conv_spec(hq), conv_spec(hk), conv_spec(hv),
            pl.BlockSpec((1, LANE), lambda bi, hi: (0, 0)),
        ],
        out_specs=pl.BlockSpec((1, s, LANE), lambda bi, hi: (bi, 0, hi)),
        out_shape=jax.ShapeDtypeStruct((b, s, heads * LANE), BF16),
        scratch_shapes=[
            pltpu.VMEM((s, LANE), F32),
            pltpu.VMEM((6, c + 2 * BF16_SUBLANES, LANE), F32),
            pltpu.VMEM((s, LANE), BF16),
            pltpu.VMEM((s, LANE), BF16),
            pltpu.VMEM((s, LANE), BF16),
            pltpu.VMEM((2, 2, g * c, LANE), BF16),
            pltpu.VMEM((2, 2, g * c, LANE), BF16),
            pltpu.VMEM((2, 2, g * c, LANE), BF16),
            pltpu.VMEM((2, 2, g * LANE, c), BF16),
            pltpu.VMEM((2, 2, g * c, c), BF16),
            pltpu.VMEM((2, 2, max(g, 8), LANE), F32),
        ],
        compiler_params=_params("parallel", "arbitrary"),
        name="gated_deltanet",
    )(p, p, p, p, gates, conv_w, conv_w, conv_w, norm_w)


def _attend_many(blocks, reach):
    scores = [_dot_nt(q, k) * (LANE ** -0.5) for (q, k, _, _, _) in blocks]
    probs, stats = [], []
    for s, (q, k, _, q0, k0) in zip(scores, blocks):
        tq, w = q.shape[0], k.shape[0]
        qpos = q0 + lax.broadcasted_iota(jnp.int32, (tq, w), 0)
        kpos = k0 + lax.broadcasted_iota(jnp.int32, (tq, w), 1)
        s = jnp.where(jnp.abs(kpos - qpos) <= reach, s, NEG_INF)
        m = jnp.max(s, axis=1, keepdims=True)
        p = jnp.exp(s - m)
        den = jnp.sum(p, axis=1, keepdims=True)
        probs.append(p.astype(BF16))
        stats.append((m, den))
    outs = [_dot(p, v) for p, (_, _, v, _, _) in zip(probs, blocks)]
    return [((o / den).astype(BF16), jnp.broadcast_to(m + jnp.log(den), (o.shape[0], LANE)))
            for o, (m, den) in zip(outs, stats)]


def _attn_kernel(q_ref, k_ref, v_ref, o_ref, lse_ref, *, length, reach, rows, dilation, unroll, residues):
    L, R, TQ, U = length, reach, ATTN_TQ, unroll
    W = TQ + 2 * R

    def residue_group(rg, carry):
        def body(step, c):
            blocks, where = [], []
            for t in range(residues):
                r = rg * residues + t
                for u in range(U):
                    i = step * U + u
                    q0 = pl.multiple_of(i * TQ, TQ)
                    k0 = pl.multiple_of(jnp.clip(q0 - R, 0, L - W), R)
                    if rows:
                        qg = pl.ds(pl.multiple_of(q0 // rows, TQ // rows), TQ // rows)
                        kg = pl.ds(pl.multiple_of(k0 // rows, R // rows), W // rows)
                        blocks.append((q_ref[0, 0, qg, r].reshape(TQ, LANE),
                                       k_ref[0, 0, kg, r].reshape(W, LANE),
                                       v_ref[0, 0, kg, r].reshape(W, LANE), q0, k0))
                    else:
                        blocks.append((q_ref[0, 0, pl.ds(q0, TQ), :], k_ref[0, 0, pl.ds(k0, W), :],
                                       v_ref[0, 0, pl.ds(k0, W), :], q0, k0))
                    where.append((r, i))
            for (r, i), (o, lse) in zip(where, _attend_many(blocks, R)):
                if rows:
                    for j in range(TQ // rows):
                        part = slice(j * rows, (j + 1) * rows)
                        o_ref[0, 0, i * (TQ // rows) + j, r] = o[part, :]
                        lse_ref[0, 0, i * (TQ // rows) + j, r] = lse[part, :]
                else:
                    q0 = pl.multiple_of(i * TQ, TQ)
                    o_ref[0, 0, pl.ds(q0, TQ), :] = o
                    lse_ref[0, 0, pl.ds(q0, TQ), :] = lse
            return c

        lax.fori_loop(0, L // TQ // U, body, 0)
        return carry

    if rows:
        lax.fori_loop(0, dilation // residues, residue_group, 0)
    else:
        residue_group(0, 0)


def _window_attention(p, hq, hk, hv, heads, window, dilation):
    b, nh, s, _ = p.shape
    reach = window // (2 * dilation)
    length = s // dilation
    unroll = math.gcd(ATTN_BLOCKS_PER_STEP, length // ATTN_TQ)
    residues = math.gcd(ATTN_BLOCKS_PER_STEP // unroll, dilation)
    assert length % ATTN_TQ == 0 and length >= ATTN_TQ + 2 * reach
    if dilation == 1:
        rows = 0
        pv = p
        out_dims = (b, heads, s, LANE)
        block = (1, 1, s, LANE)
        tail = (0, 0)
    else:
        rows = PERM_ROWS // dilation
        assert rows % BF16_SUBLANES == 0 and ATTN_TQ % rows == 0 and reach % rows == 0 and s % PERM_ROWS == 0
        ng = s // PERM_ROWS
        pv = p.reshape(b, nh, ng, dilation, rows, LANE)
        out_dims = (b, heads, ng, dilation, rows, LANE)
        block = (1, 1, ng, dilation, rows, LANE)
        tail = (0, 0, 0, 0)

    def spec(off):
        return pl.BlockSpec(block, lambda bi, hi, off=off: (bi, off + hi) + tail)

    kern = functools.partial(_attn_kernel, length=length, reach=reach, rows=rows, dilation=dilation,
                             unroll=unroll, residues=residues)
    o, lse = pl.pallas_call(
        kern,
        grid=(b, heads),
        in_specs=[spec(hq), spec(hk), spec(hv)],
        out_specs=[spec(0), spec(0)],
        out_shape=[jax.ShapeDtypeStruct(out_dims, BF16), jax.ShapeDtypeStruct(out_dims, F32)],
        compiler_params=_params("parallel", "parallel"),
        name=f"window_attention_d{dilation}",
    )(pv, pv, pv)
    return o.reshape(b, heads, s, LANE), lse.reshape(b, heads, s, LANE)


def _select_rows_packed(sel, xs):
    assert 3 * len(xs) <= LANE
    lane = lax.broadcasted_iota(jnp.int32, (1, LANE), 1)
    packed = jnp.zeros(xs[0].shape, F32)
    for a, x in enumerate(xs):
        hi = x.astype(BF16).astype(F32)
        rest = x - hi
        mid = rest.astype(BF16).astype(F32)
        lo = (rest - mid).astype(BF16).astype(F32)
        for t, term in enumerate((hi, mid, lo)):
            packed = jnp.where(lane == 3 * a + t, term, packed)
    moved = _dot(sel, packed.astype(BF16))
    outs = []
    for a in range(len(xs)):
        mine = jnp.logical_and(lane >= 3 * a, lane < 3 * a + 3)
        col = jnp.sum(jnp.where(mine, moved, 0.0), axis=1, keepdims=True)
        outs.append(jnp.broadcast_to(col, moved.shape))
    return outs


def _combine_kernel(*refs, heads, dilations):
    ng = len(dilations)
    o_refs, l_refs = refs[:ng], refs[ng:2 * ng]
    z_ref = refs[2 * ng]
    unperm_refs = refs[2 * ng + 1:-1]
    y_ref = refs[-1]
    unperm_of = {}
    for d in dilations:
        if d > 1:
            unperm_of[d] = unperm_refs[len(unperm_of)]
    lse_of = {}
    for g, d in enumerate(dilations):
        if d > 1:
            lse_of[g] = _select_rows_packed(unperm_of[d][...], [l_refs[g][0, h] for h in range(heads)])
    for h in range(heads):
        outs, lses = [], []
        for g, d in enumerate(dilations):
            if d > 1:
                outs.append(_dot(unperm_of[d][...], o_refs[g][0, h]))
                lses.append(lse_of[g][h])
            else:
                outs.append(o_refs[g][0, h].astype(F32))
                lses.append(l_refs[g][0, h])
        m = functools.reduce(jnp.maximum, lses)
        wts = [jnp.exp(l - m) for l in lses]
        den = functools.reduce(lambda a, c: a + c, wts)
        acc = functools.reduce(lambda a, c: a + c, [w * o for w, o in zip(wts, outs)])
        z = z_ref[0, h].astype(F32)
        y_ref[0, :, h * LANE:(h + 1) * LANE] = (acc / den * _silu(z)).astype(BF16)


def _combine(outs, lses, unperms, p, hz, heads):
    b, _, s, _ = p.shape
    ts = PERM_ROWS
    assert hz % heads == 0 and s % ts == 0
    dilations = tuple(dil for _, dil in B_GROUPS)
    spec = pl.BlockSpec((1, heads, ts, LANE), lambda bi, i: (bi, 0, i, 0))
    n = len(outs)
    return pl.pallas_call(
        functools.partial(_combine_kernel, heads=heads, dilations=dilations),
        grid=(b, s // ts),
        in_specs=[spec] * (2 * n)
        + [pl.BlockSpec((1, heads, ts, LANE), lambda bi, i: (bi, hz // heads, i, 0))]
        + [pl.BlockSpec((PERM_ROWS, PERM_ROWS), lambda bi, i: (0, 0))] * len(unperms),
        out_specs=pl.BlockSpec((1, ts, heads * LANE), lambda bi, i: (bi, i, 0)),
        out_shape=jax.ShapeDtypeStruct((b, s, heads * LANE), BF16),
        compiler_params=_params("parallel", "parallel"),
        name="group_combine",
    )(*outs, *lses, p, *unperms)


def _merge_kernel(ya_ref, yb_ref, wa_ref, wb_ref, ga_ref, gb_ref, o_ref, *, heads_per_tile):
    acc_a = _dot(ya_ref[0], wa_ref[...])
    acc_b = _dot(yb_ref[0], wb_ref[...])
    for k in range(heads_per_tile):
        sl = slice(k * LANE, (k + 1) * LANE)
        ga = jax.nn.sigmoid(ga_ref[0, k].astype(F32))
        gb = jax.nn.sigmoid(gb_ref[0, k].astype(F32))
        o_ref[0, :, sl] = (ga * acc_a[:, sl] + gb * acc_b[:, sl]).astype(BF16)


def _merge(ya, yb, wa, wb, p, hga, hgb):
    b, s, ka = ya.shape
    kb = yb.shape[2]
    d = wa.shape[1]
    tm = _largest_tile(s, (1024, 512, 256, 128))
    tn = _largest_tile(math.gcd(d, hga * LANE, hgb * LANE), (1024, 512, 256, 128))
    hpt = tn // LANE
    return pl.pallas_call(
        functools.partial(_merge_kernel, heads_per_tile=hpt),
        grid=(b, s // tm, d // tn),
        in_specs=[
            pl.BlockSpec((1, tm, ka), lambda bi, i, j: (bi, i, 0)),
            pl.BlockSpec((1, tm, kb), lambda bi, i, j: (bi, i, 0)),
            pl.BlockSpec((ka, tn), lambda bi, i, j: (0, j)),
            pl.BlockSpec((kb, tn), lambda bi, i, j: (0, j)),
            pl.BlockSpec((1, hpt, tm, LANE), lambda bi, i, j: (bi, hga // hpt + j, i, 0)),
            pl.BlockSpec((1, hpt, tm, LANE), lambda bi, i, j: (bi, hgb // hpt + j, i, 0)),
        ],
        out_specs=pl.BlockSpec((1, tm, tn), lambda bi, i, j: (bi, i, j)),
        out_shape=jax.ShapeDtypeStruct((b, s, d), BF16),
        compiler_params=_params("parallel", "parallel", "arbitrary"),
        name="branch_merge",
    )(ya, yb, wa, wb, p, p)


def _final_kernel(*refs, with_next):
    if with_next:
        x_ref, out_ref, gate_ref, gpost_ref, sh_ref, sc_ref, gpre_ref, y_ref, h_ref = refs
    else:
        x_ref, out_ref, gate_ref, gpost_ref, y_ref = refs
    out = out_ref[0]
    normed = out * lax.rsqrt(jnp.mean(out * out, axis=-1, keepdims=True) + EPS) * gpost_ref[...]
    y = x_ref[0] + gate_ref[0] * normed
    y_ref[0] = y
    if with_next:
        h_ref[0] = _modulated_norm(y, gpre_ref[...], sc_ref[0], sh_ref[0]).astype(BF16)


def _finalize(x, out, mod, g_post, next_mod, next_gain):
    b, s, d = x.shape
    ts = _largest_tile(s, (256, 128, 64, 32, 16, 8))
    with_next = next_mod is not None
    tile = pl.BlockSpec((1, ts, d), lambda bi, i: (bi, i, 0))
    row = pl.BlockSpec((1, d), lambda bi, i: (0, 0))

    def mod_spec(part):
        return pl.BlockSpec((1, 1, d), lambda bi, i, part=part: (bi, 0, part))

    in_specs = [tile, tile, mod_spec(2), row]
    args = [x, out, mod, g_post.reshape(1, d)]
    out_specs = [tile]
    out_shape = [jax.ShapeDtypeStruct((b, s, d), F32)]
    if with_next:
        in_specs += [mod_spec(0), mod_spec(1), row]
        args += [next_mod, next_mod, next_gain.reshape(1, d)]
        out_specs.append(tile)
        out_shape.append(jax.ShapeDtypeStruct((b, s, d), BF16))
    res = pl.pallas_call(
        functools.partial(_final_kernel, with_next=with_next),
        grid=(b, s // ts),
        in_specs=in_specs,
        out_specs=out_specs,
        out_shape=out_shape,
        compiler_params=_params("parallel", "parallel"),
        name="postnorm_residual",
    )(*args)
    return (res[0], res[1]) if with_next else (res[0], None)


def _rope_tables(s):
    half = LANE // 2
    inv = ROPE_THETA ** (-jnp.arange(half, dtype=F32) / half)
    ang = jnp.arange(s, dtype=F32)[:, None] * inv[None, :]
    cos, sin = jnp.cos(ang), jnp.sin(ang)
    return jnp.concatenate([cos, cos], axis=1), jnp.concatenate([-sin, sin], axis=1)


def _pad_row(v):
    v = v.reshape(1, -1).astype(F32)
    return jnp.pad(v, ((0, 0), (0, LANE - v.shape[1])))


def kernel(x_prompt, x_sample, c_prompt, c_sample, w_ada, b_ada, norm_pre, norm_post, w_in, conv_a,
           a_log, dt_bias, norm_a, w_up_a, w_up_b, w_out):
    depth, d, _ = w_in.shape
    ah = d // 256
    bh = d // 512
    aw = ah * LANE
    n_ab = 4 * ah
    assert n_ab <= LANE
    hq_b = 4 * ah
    hk_b = hq_b + 3 * bh
    hv_b = hk_b + 3 * bh
    hz_b = hv_b + 3 * bh
    hg_a = hz_b + bh
    hg_b = hg_a + d // LANE
    c0 = 4 * aw
    c1 = c0 + n_ab

    bp, bs = c_prompt.shape[0], c_sample.shape[0]
    c_all = jnp.concatenate([c_prompt, c_sample], axis=0)
    c_all = jnp.pad(c_all, ((0, -c_all.shape[0] % 8), (0, 0)))
    mod_all = _modulation(c_all, w_ada, b_ada)

    w_main = [jnp.concatenate([w_in[l, :, :c0], w_in[l, :, c1:]], axis=1).astype(BF16) for l in range(depth)]
    w_ab = [jnp.pad(w_in[l, :, c0:c1], ((0, 0), (0, LANE - n_ab))).astype(BF16) for l in range(depth)]
    wa = [w_up_a[l].astype(BF16) for l in range(depth)]
    wb = [w_up_b[l].astype(BF16) for l in range(depth)]
    wo = [w_out[l].astype(BF16) for l in range(depth)]
    perms = [_residue_permutation(dil) for _, dil in B_GROUPS if dil > 1]
    unperms = [pm.T for pm in perms]

    def trunk(x, row0, nb):
        s = x.shape[1]
        cos_t, sin_t = _rope_tables(s)
        mods = [mod_all[l, row0:row0 + nb].reshape(nb, 1, 3 * d) for l in range(depth)]
        h = _prenorm(x, mods[0], norm_pre[0])
        for l in range(depth):
            p = _inproj(h, w_main[l], cos_t, sin_t, perms, hq_b, hv_b, hz_b, bh)
            ab = _matmul(h, w_ab[l], F32, "in_proj_gates")
            gates = _gates(ab, _pad_row(a_log[l]), _pad_row(dt_bias[l]), ah)
            ya = _gdn(p, gates, conv_a[l], norm_a[l].reshape(1, LANE), ah)
            outs, lses = [], []
            for gi, (window, dilation) in enumerate(B_GROUPS):
                o, lse = _window_attention(p, hq_b + gi * bh, hk_b + gi * bh, hv_b + gi * bh, bh,
                                           window, dilation)
                outs.append(o)
                lses.append(lse)
            yb = _combine(outs, lses, unperms, p, hz_b, bh)
            merged = _merge(ya, yb, wa[l], wb[l], p, hg_a, hg_b)
            out = _matmul(merged, wo[l], F32, "out_proj")
            last = l == depth - 1
            x, h = _finalize(x, out, mods[l], norm_post[l],
                             None if last else mods[l + 1], None if last else norm_pre[l + 1])
        return x

    y_prompt = trunk(x_prompt, 0, bp)
    y_sample = trunk(x_sample, bp, bs)
    return (y_prompt, y_sample)
```
